```python
import math
import jax, jax.numpy as jnp
from jax import lax
import numpy as np

D_MODEL = 2048
BATCH = 2
SEQ = 4096
DEPTH = 4
DEC_BATCH = 128
DEC_SEQ = 4
PAST_LEN = 8192
PAGE_SIZE = 128

N_MIXERS = 2
N_CONV_LAYERS = (DEPTH + 1) // 2
N_ATTN_LAYERS = DEPTH // 2
CONV_WIDTH = 31
N_HEADS = 16
Q_LORA = 512
KV_LORA = 512
NOPE_DIM = 128
ROPE_DIM = 64
V_DIM = 128
D_FF = 5632
FFN_CONV_WIDTH = 3
ROPE_BASE = 10000.0
Q_BLOCK = 128
EPS = 1e-6
ATTN_SCALE = 1.0 / math.sqrt(NOPE_DIM + ROPE_DIM)

kernel_name = 'hybrid_conformer_mla_convffn_step'


def rms_norm(x, g):
    xf = x.astype(jnp.float32)
    y = xf * lax.rsqrt(jnp.mean(xf * xf, axis=-1, keepdims=True) + EPS)
    return (y * g.astype(jnp.float32)).astype(x.dtype)


def layer_norm(x, g, b):
    xf = x.astype(jnp.float32)
    mu = jnp.mean(xf, axis=-1, keepdims=True)
    xc = xf - mu
    var = jnp.mean(xc * xc, axis=-1, keepdims=True)
    y = xc * lax.rsqrt(var + EPS) * g.astype(jnp.float32) + b.astype(jnp.float32)
    return y.astype(x.dtype)


def causal_dwconv(u_ext, w, b):
    out = lax.conv_general_dilated(u_ext, w[:, None, :].astype(u_ext.dtype), window_strides=(1,),
                                   padding='VALID', dimension_numbers=('NWC', 'WIO', 'NWC'),
                                   feature_group_count=u_ext.shape[-1])
    return out + b


def rope_angles(pos):
    inv_freq = ROPE_BASE ** (-jnp.arange(0, ROPE_DIM, 2, dtype=jnp.float32) / ROPE_DIM)
    ang = pos.astype(jnp.float32)[:, None] * inv_freq[None, :]
    return jnp.cos(ang), jnp.sin(ang)


def apply_rope(x, cos, sin):
    xf = x.astype(jnp.float32)
    x1, x2 = xf[..., :ROPE_DIM // 2], xf[..., ROPE_DIM // 2:]
    return jnp.concatenate([x1 * cos - x2 * sin, x2 * cos + x1 * sin], axis=-1).astype(x.dtype)


def conformer_conv(h, buf, w_in, b_in, dw, dw_b, ln_g, ln_b, w_out, b_out):
    a = h @ w_in + b_in
    u = a[..., :D_MODEL] * jax.nn.sigmoid(a[..., D_MODEL:])
    u_ext = jnp.concatenate([buf.astype(u.dtype), u], axis=1)
    c = layer_norm(causal_dwconv(u_ext, dw, dw_b), ln_g, ln_b)
    y = jax.nn.silu(c) @ w_out + b_out
    return y, u_ext[:, -(CONV_WIDTH - 1):]


def conv_ffn(h, buf, w_gate, w_up, dw, dw_b, w_down):
    g_ext = jnp.concatenate([buf.astype(h.dtype), h @ w_gate], axis=1)
    gc = causal_dwconv(g_ext, dw, dw_b)
    y = (jax.nn.silu(gc) * (h @ w_up)) @ w_down
    return y, g_ext[:, -(FFN_CONV_WIDTH - 1):]


def mla_project(h, pos, w_in, q_norm, kv_norm, w_qb):
    z = h @ w_in
    q_a = rms_norm(z[..., :Q_LORA], q_norm)
    c_kv = rms_norm(z[..., Q_LORA:Q_LORA + KV_LORA], kv_norm)
    k_r = z[..., Q_LORA + KV_LORA:]
    q = jnp.einsum('nsq,qhe->nshe', q_a, w_qb)
    cos, sin = rope_angles(pos)
    q_nope = q[..., :NOPE_DIM]
    q_rope = apply_rope(q[..., NOPE_DIM:], cos[:, None, :], sin[:, None, :])
    k_rope = apply_rope(k_r, cos, sin)
    return q_nope, q_rope, c_kv, k_rope


def mla_prompt(h, w_in, q_norm, kv_norm, w_qb, w_uk, w_uv, w_o):
    n, s, _ = h.shape
    pos = jnp.arange(s)
    q_nope, q_rope, c_kv, k_rope = mla_project(h, pos, w_in, q_norm, kv_norm, w_qb)
    k_nope = jnp.einsum('nkr,rhd->nkhd', c_kv, w_uk)
    v = jnp.einsum('nkr,rhv->nkhv', c_kv, w_uv)
    nb = s // Q_BLOCK

    def to_blocks(t):
        return t.reshape((n, nb, Q_BLOCK) + t.shape[2:]).swapaxes(0, 1)

    def block(args):
        qn, qr, qpos = args
        sc = (jnp.einsum('nqhd,nkhd->nhqk', qn, k_nope)
              + jnp.einsum('nqhe,nke->nhqk', qr, k_rope)).astype(jnp.float32) * ATTN_SCALE
        mask = pos[None, :] <= qpos[:, None]
        p = jax.nn.softmax(jnp.where(mask, sc, -jnp.inf), axis=-1)
        return jnp.einsum('nhqk,nkhv->nqhv', p.astype(v.dtype), v)

    o = lax.map(block, (to_blocks(q_nope), to_blocks(q_rope), pos.reshape(nb, Q_BLOCK)))
    o = o.swapaxes(0, 1).reshape(n, s, N_HEADS, V_DIM)
    y = jnp.einsum('nshv,hvd->nsd', o, w_o)
    return y, c_kv, k_rope


def mla_sample(h, cache_c, cache_r, page_table, w_in, q_norm, kv_norm, w_qb, w_uk, w_uv, w_o):
    n, s, _ = h.shape
    q_pos = PAST_LEN + jnp.arange(s)
    q_nope, q_rope, c_kv, k_rope = mla_project(h, q_pos, w_in, q_norm, kv_norm, w_qb)
    q_lat = jnp.einsum('nqhd,rhd->nqhr', q_nope, w_uk)
    k_pos = jnp.arange(PAST_LEN + s)
    mask = k_pos[None, :] <= q_pos[:, None]

    def one_seq(args):
        ql, qr, cn, rn, pages = args
        c_all = jnp.concatenate([cache_c[pages].reshape(-1, KV_LORA).astype(cn.dtype), cn], axis=0)
        r_all = jnp.concatenate([cache_r[pages].reshape(-1, ROPE_DIM).astype(rn.dtype), rn], axis=0)
        sc = (jnp.einsum('qhr,kr->hqk', ql, c_all)
              + jnp.einsum('qhe,ke->hqk', qr, r_all)).astype(jnp.float32) * ATTN_SCALE
        p = jax.nn.softmax(jnp.where(mask, sc, -jnp.inf), axis=-1)
        return jnp.einsum('hqk,kr->qhr', p.astype(c_all.dtype), c_all)

    o_lat = lax.map(one_seq, (q_lat, q_rope, c_kv, k_rope, page_table))
    o = jnp.einsum('nqhr,rhv->nqhv', o_lat, w_uv)
    y = jnp.einsum('nqhv,hvd->nqd', o, w_o)
    return y, c_kv, k_rope


def trunk(x, conv_state, ffn_state, attn_fn, norms, cv, ffn):
    n_pre, n_post, f_pre, f_post = norms
    kv_rows, kr_rows, conv_out, ffn_out = [], [], [], []
    for i in range(DEPTH):
        j = i // N_MIXERS
        h = rms_norm(x, n_pre[i])
        if i % N_MIXERS == 0:
            y, buf = conformer_conv(h, conv_state[j], *(w[j] for w in cv))
            conv_out.append(buf)
        else:
            y, c_new, r_new = attn_fn(h, j)
            kv_rows.append(c_new)
            kr_rows.append(r_new)
        x = x + rms_norm(y, n_post[i])
        h = rms_norm(x, f_pre[i])
        y, buf = conv_ffn(h, ffn_state[i], *(w[i] for w in ffn))
        ffn_out.append(buf)
        x = x + rms_norm(y, f_post[i])
    return x, jnp.stack(kv_rows), jnp.stack(kr_rows), jnp.stack(conv_out), jnp.stack(ffn_out)


def setup_inputs(seed: int = 0) -> dict:
    key = jax.random.key(seed)
    ks = iter(jax.random.split(key, 40))

    def nrm(shape, scale):
        return jax.random.normal(next(ks), shape, jnp.float32) * scale

    def gain(shape):
        return 1.0 + nrm(shape, 0.05)

    n_pages = PAST_LEN // PAGE_SIZE
    n_used = DEC_BATCH * n_pages
    n_phys = n_used + n_used // 4
    page_table = jax.random.permutation(next(ks), n_phys)[:n_used].reshape(DEC_BATCH, n_pages).astype(jnp.int32)
    d_in = Q_LORA + KV_LORA + ROPE_DIM
    return {
        'x_prompt': nrm((BATCH, SEQ, D_MODEL), 1.0),
        'x_sample': nrm((DEC_BATCH, DEC_SEQ, D_MODEL), 1.0),
        'cache_kv_latent': nrm((N_ATTN_LAYERS, n_phys, PAGE_SIZE, KV_LORA), 1.0),
        'cache_k_rope': nrm((N_ATTN_LAYERS, n_phys, PAGE_SIZE, ROPE_DIM), 1.0),
        'state_conv': nrm((N_CONV_LAYERS, DEC_BATCH, CONV_WIDTH - 1, D_MODEL), 0.5),
        'state_ffn_conv': nrm((DEPTH, DEC_BATCH, FFN_CONV_WIDTH - 1, D_FF), 1.0),
        'page_table': page_table,
        'norm_mix_pre': gain((DEPTH, D_MODEL)),
        'norm_mix_post': gain((DEPTH, D_MODEL)),
        'norm_ffn_pre': gain((DEPTH, D_MODEL)),
        'norm_ffn_post': gain((DEPTH, D_MODEL)),
        'cv_w_in': nrm((N_CONV_LAYERS, D_MODEL, 2 * D_MODEL), D_MODEL ** -0.5),
        'cv_b_in': nrm((N_CONV_LAYERS, 2 * D_MODEL), 0.02),
        'cv_dw': nrm((N_CONV_LAYERS, CONV_WIDTH, D_MODEL), CONV_WIDTH ** -0.5),
        'cv_dw_b': nrm((N_CONV_LAYERS, D_MODEL), 0.02),
        'cv_ln_g': gain((N_CONV_LAYERS, D_MODEL)),
        'cv_ln_b': nrm((N_CONV_LAYERS, D_MODEL), 0.02),
        'cv_w_out': nrm((N_CONV_LAYERS, D_MODEL, D_MODEL), D_MODEL ** -0.5),
        'cv_b_out': nrm((N_CONV_LAYERS, D_MODEL), 0.02),
        'mla_w_in': nrm((N_ATTN_LAYERS, D_MODEL, d_in), D_MODEL ** -0.5),
        'mla_q_norm': gain((N_ATTN_LAYERS, Q_LORA)),
        'mla_kv_norm': gain((N_ATTN_LAYERS, KV_LORA)),
        'mla_w_qb': nrm((N_ATTN_LAYERS, Q_LORA, N_HEADS, NOPE_DIM + ROPE_DIM), Q_LORA ** -0.5),
        'mla_w_uk': nrm((N_ATTN_LAYERS, KV_LORA, N_HEADS, NOPE_DIM), KV_LORA ** -0.5),
        'mla_w_uv': nrm((N_ATTN_LAYERS, KV_LORA, N_HEADS, V_DIM), KV_LORA ** -0.5),
        'mla_w_o': nrm((N_ATTN_LAYERS, N_HEADS, V_DIM, D_MODEL), (N_HEADS * V_DIM) ** -0.5),
        'ffn_w_gate': nrm((DEPTH, D_MODEL, D_FF), D_MODEL ** -0.5),
        'ffn_w_up': nrm((DEPTH, D_MODEL, D_FF), D_MODEL ** -0.5),
        'ffn_dw': nrm((DEPTH, FFN_CONV_WIDTH, D_FF), FFN_CONV_WIDTH ** -0.5),
        'ffn_dw_b': nrm((DEPTH, D_FF), 0.02),
        'ffn_w_down': nrm((DEPTH, D_FF, D_MODEL), D_FF ** -0.5),
    }


def reference(x_prompt, x_sample, cache_kv_latent, cache_k_rope, state_conv, state_ffn_conv, page_table,
              norm_mix_pre, norm_mix_post, norm_ffn_pre, norm_ffn_post,
              cv_w_in, cv_b_in, cv_dw, cv_dw_b, cv_ln_g, cv_ln_b, cv_w_out, cv_b_out,
              mla_w_in, mla_q_norm, mla_kv_norm, mla_w_qb, mla_w_uk, mla_w_uv, mla_w_o,
              ffn_w_gate, ffn_w_up, ffn_dw, ffn_dw_b, ffn_w_down):
    norms = (norm_mix_pre, norm_mix_post, norm_ffn_pre, norm_ffn_post)
    cv = (cv_w_in, cv_b_in, cv_dw, cv_dw_b, cv_ln_g, cv_ln_b, cv_w_out, cv_b_out)
    mla = (mla_w_in, mla_q_norm, mla_kv_norm, mla_w_qb, mla_w_uk, mla_w_uv, mla_w_o)
    ffn = (ffn_w_gate, ffn_w_up, ffn_dw, ffn_dw_b, ffn_w_down)

    def prompt_attn(h, j):
        return mla_prompt(h, *(w[j] for w in mla))

    def sample_attn(h, j):
        return mla_sample(h, cache_kv_latent[j], cache_k_rope[j], page_table, *(w[j] for w in mla))

    bp = x_prompt.shape[0]
    conv0 = jnp.zeros((N_CONV_LAYERS, bp, CONV_WIDTH - 1, D_MODEL), x_prompt.dtype)
    ffn0 = jnp.zeros((DEPTH, bp, FFN_CONV_WIDTH - 1, D_FF), x_prompt.dtype)
    y_prompt, kv_p, kr_p, conv_p, ffn_p = trunk(x_prompt, conv0, ffn0, prompt_attn, norms, cv, ffn)
    y_sample, kv_s, kr_s, conv_s, ffn_s = trunk(x_sample, state_conv, state_ffn_conv, sample_attn, norms, cv, ffn)
    return (y_prompt, y_sample, kv_p, kr_p, kv_s, kr_s, conv_p, conv_s, ffn_p, ffn_s)
```

```python
import functools
import math

import jax
import jax.numpy as jnp
from jax import lax
from jax.experimental import pallas as pl
from jax.experimental.pallas import tpu as pltpu

EPS = 1e-6
ROPE_BASE = 10000.0
F32 = jnp.float32
BF16 = jnp.bfloat16

V7X_VMEM_LIMIT_BYTES = 56 * 1024 * 1024


def _params(*sem):
    return pltpu.CompilerParams(dimension_semantics=sem, vmem_limit_bytes=V7X_VMEM_LIMIT_BYTES)


def _dot(a, b):
    return jnp.dot(a, b, preferred_element_type=F32)


def _dot_nt(a, b):
    return lax.dot_general(a, b, (((1,), (1,)), ((), ())), preferred_element_type=F32)


def _rms(xf, g):
    ms = jnp.mean(xf * xf, axis=-1, keepdims=True)
    return (xf * lax.rsqrt(ms + EPS)) * g


def _silu(x):
    return x * jax.nn.sigmoid(x)


def _conv_in_kernel(x_ref, g_ref, w1_ref, w2_ref, b1_ref, b2_ref, u_ref, h_ref):
    @pl.when(pl.program_id(1) == 0)
    def _():
        h_ref[...] = _rms(x_ref[...], g_ref[...]).astype(BF16)

    h = h_ref[...]
    a1 = _dot(h, w1_ref[...]) + b1_ref[...]
    a2 = _dot(h, w2_ref[...]) + b2_ref[...]
    u_ref[...] = a1 * jax.nn.sigmoid(a2)


def conv_in(x, g_pre, w_in, b_in, *, tm, tn):
    t, d = x.shape
    nj = d // tn
    return pl.pallas_call(
        _conv_in_kernel,
        grid=(t // tm, nj),
        in_specs=[
            pl.BlockSpec((tm, d), lambda i, j: (i, 0)),
            pl.BlockSpec((1, d), lambda i, j: (0, 0)),
            pl.BlockSpec((d, tn), lambda i, j: (0, j)),
            pl.BlockSpec((d, tn), lambda i, j: (0, j + nj)),
            pl.BlockSpec((1, tn), lambda i, j: (0, j)),
            pl.BlockSpec((1, tn), lambda i, j: (0, j + nj)),
        ],
        out_specs=pl.BlockSpec((tm, tn), lambda i, j: (i, j)),
        out_shape=jax.ShapeDtypeStruct((t, d), F32),
        scratch_shapes=[pltpu.VMEM((tm, d), BF16)],
        compiler_params=_params("parallel", "arbitrary"),
        name="conv_in",
    )(x, g_pre, w_in, w_in, b_in, b_in)


CONV_HALO = 32


def _conv_out_prompt_kernel(u_ref, uh_ref, x_ref, dw_ref, dwb_ref, lng_ref, lnb_ref, w_ref, b_ref, gp_ref,
                            o_ref, ext_ref, c_ref, *, tiles_per_seq, width, lane_chunk):
    tm, d = u_ref.shape
    first = pl.program_id(0) % tiles_per_seq == 0
    ext_ref[0:CONV_HALO, :] = jnp.where(first, 0.0, uh_ref[...])
    ext_ref[CONV_HALO:, :] = u_ref[...]
    base = CONV_HALO - (width - 1)

    def chunk(c, carry):
        col = pl.ds(pl.multiple_of(c * lane_chunk, lane_chunk), lane_chunk)
        acc = jnp.zeros((tm, lane_chunk), F32)
        for k in range(width):
            acc = acc + ext_ref[pl.ds(base + k, tm), col] * dw_ref[pl.ds(k, 1), col]
        c_ref[:, col] = acc + dwb_ref[:, col]
        return carry

    lax.fori_loop(0, d // lane_chunk, chunk, 0)
    c = c_ref[...]
    mu = jnp.mean(c, axis=-1, keepdims=True)
    xc = c - mu
    var = jnp.mean(xc * xc, axis=-1, keepdims=True)
    y = xc * lax.rsqrt(var + EPS) * lng_ref[...] + lnb_ref[...]
    z = _dot(_silu(y).astype(BF16), w_ref[...]) + b_ref[...]
    o_ref[...] = x_ref[...] + _rms(z, gp_ref[...])


def conv_out_prompt(u, x, dw, dw_b, ln_g, ln_b, w_out, b_out, g_post, *, seq, tm):
    t, d = u.shape
    width = dw.shape[0]
    hb = tm // CONV_HALO
    row = lambda i: (i, 0)
    fixed = lambda i: (0, 0)
    kern = functools.partial(_conv_out_prompt_kernel, tiles_per_seq=seq // tm, width=width, lane_chunk=128)
    return pl.pallas_call(
        kern,
        grid=(t // tm,),
        in_specs=[
            pl.BlockSpec((tm, d), row),
            pl.BlockSpec((CONV_HALO, d), lambda i: (jnp.maximum(i * hb - 1, 0), 0)),
            pl.BlockSpec((tm, d), row),
            pl.BlockSpec((width, d), fixed),
            pl.BlockSpec((1, d), fixed),
            pl.BlockSpec((1, d), fixed),
            pl.BlockSpec((1, d), fixed),
            pl.BlockSpec((d, d), fixed),
            pl.BlockSpec((1, d), fixed),
            pl.BlockSpec((1, d), fixed),
        ],
        out_specs=pl.BlockSpec((tm, d), row),
        out_shape=jax.ShapeDtypeStruct((t, d), F32),
        scratch_shapes=[pltpu.VMEM((tm + CONV_HALO, d), F32), pltpu.VMEM((tm, d), F32)],
        compiler_params=_params("parallel"),
        name="conv_out_prompt",
    )(u, u, x, dw, dw_b, ln_g, ln_b, w_out, b_out, g_post)


def _conv_out_sample_kernel(ue_ref, x_ref, dw_ref, dwb_ref, lng_ref, lnb_ref, w_ref, b_ref, gp_ref, o_ref,
                            *, width):
    n_tok, ns, d = x_ref.shape
    acts = []
    for t in range(n_tok):
        c = jnp.zeros((ns, d), F32)
        for k in range(width):
            c = c + ue_ref[t + k] * dw_ref[pl.ds(k, 1), :]
        c = c + dwb_ref[...]
        mu = jnp.mean(c, axis=-1, keepdims=True)
        xc = c - mu
        var = jnp.mean(xc * xc, axis=-1, keepdims=True)
        y = xc * lax.rsqrt(var + EPS) * lng_ref[...] + lnb_ref[...]
        acts.append(_silu(y).astype(BF16))
    z = _dot(jnp.concatenate(acts, axis=0), w_ref[...]) + b_ref[...]
    r = _rms(z, gp_ref[...])
    for t in range(n_tok):
        o_ref[t] = x_ref[t] + r[t * ns:(t + 1) * ns]


def conv_out_sample(u_ext, x, dw, dw_b, ln_g, ln_b, w_out, b_out, g_post, *, ns):
    rows, n, d = u_ext.shape
    n_tok = x.shape[0]
    width = dw.shape[0]
    fixed = lambda i: (0, 0)
    return pl.pallas_call(
        functools.partial(_conv_out_sample_kernel, width=width),
        grid=(n // ns,),
        in_specs=[
            pl.BlockSpec((rows, ns, d), lambda i: (0, i, 0)),
            pl.BlockSpec((n_tok, ns, d), lambda i: (0, i, 0)),
            pl.BlockSpec((width, d), fixed),
            pl.BlockSpec((1, d), fixed),
            pl.BlockSpec((1, d), fixed),
            pl.BlockSpec((1, d), fixed),
            pl.BlockSpec((d, d), fixed),
            pl.BlockSpec((1, d), fixed),
            pl.BlockSpec((1, d), fixed),
        ],
        out_specs=pl.BlockSpec((n_tok, ns, d), lambda i: (0, i, 0)),
        out_shape=jax.ShapeDtypeStruct(x.shape, F32),
        compiler_params=_params("parallel"),
        name="conv_out_sample",
    )(u_ext, x, dw, dw_b, ln_g, ln_b, w_out, b_out, g_post)


FFN_TAIL = 8


def _ffn_finish(x_ref, gpost_ref, acc_ref, o_ref):
    o_ref[...] = x_ref[...] + _rms(acc_ref[...], gpost_ref[...])


def _ffn_prompt_kernel(x_ref, gpre_ref, gpost_ref, wg_ref, wu_ref, dw_ref, dwb_ref, wd_ref,
                       o_ref, gt_ref, h_ref, acc_ref, carry_ref, *, tiles_per_seq):
    i, f = pl.program_id(0), pl.program_id(1)
    tm = x_ref.shape[0]

    @pl.when(f == 0)
    def _():
        h_ref[...] = _rms(x_ref[...], gpre_ref[...]).astype(BF16)
        acc_ref[...] = jnp.zeros_like(acc_ref)

    @pl.when(i % tiles_per_seq == 0)
    def _():
        carry_ref[f] = jnp.zeros(carry_ref.shape[1:], F32)

    h = h_ref[...]
    g = _dot(h, wg_ref[...])
    u = _dot(h, wu_ref[...])
    prev = carry_ref[f]
    p1 = prev[FFN_TAIL - 1:FFN_TAIL]
    p2 = prev[FFN_TAIL - 2:FFN_TAIL - 1]
    row = lax.broadcasted_iota(jnp.int32, g.shape, 0)
    g1 = jnp.where(row == 0, p1, pltpu.roll(g, 1, 0))
    g2 = jnp.where(row == 0, p2, jnp.where(row == 1, p1, pltpu.roll(g, 2, 0)))
    gc = g2 * dw_ref[0:1, :] + g1 * dw_ref[1:2, :] + g * dw_ref[2:3, :] + dwb_ref[...]
    act = (_silu(gc) * u).astype(BF16)
    acc_ref[...] += _dot(act, wd_ref[...])
    tail = g[tm - FFN_TAIL:, :]
    carry_ref[f] = tail
    gt_ref[...] = tail

    @pl.when(f == pl.num_programs(1) - 1)
    def _():
        _ffn_finish(x_ref, gpost_ref, acc_ref, o_ref)


def ffn_prompt(x, g_pre, g_post, w_gate, w_up, dw, dw_b, w_down, *, seq, tm, tf):
    t, d = x.shape
    dff = w_gate.shape[1]
    nm, nf = t // tm, dff // tf
    kern = functools.partial(_ffn_prompt_kernel, tiles_per_seq=seq // tm)
    return pl.pallas_call(
        kern,
        grid=(nm, nf),
        in_specs=[
            pl.BlockSpec((tm, d), lambda i, f: (i, 0)),
            pl.BlockSpec((1, d), lambda i, f: (0, 0)),
            pl.BlockSpec((1, d), lambda i, f: (0, 0)),
            pl.BlockSpec((d, tf), lambda i, f: (0, f)),
            pl.BlockSpec((d, tf), lambda i, f: (0, f)),
            pl.BlockSpec((dw.shape[0], tf), lambda i, f: (0, f)),
            pl.BlockSpec((1, tf), lambda i, f: (0, f)),
            pl.BlockSpec((tf, d), lambda i, f: (f, 0)),
        ],
        out_specs=[
            pl.BlockSpec((tm, d), lambda i, f: (i, 0)),
            pl.BlockSpec((FFN_TAIL, tf), lambda i, f: (i, f)),
        ],
        out_shape=[jax.ShapeDtypeStruct((t, d), F32), jax.ShapeDtypeStruct((nm * FFN_TAIL, dff), F32)],
        scratch_shapes=[pltpu.VMEM((tm, d), BF16), pltpu.VMEM((tm, d), F32),
                        pltpu.VMEM((nf, FFN_TAIL, tf), F32)],
        compiler_params=_params("arbitrary", "arbitrary"),
        name="ffn_prompt",
    )(x, g_pre, g_post, w_gate, w_up, dw, dw_b, w_down)


def _ffn_sample_kernel(x_ref, gpre_ref, gpost_ref, st_ref, wg_ref, wu_ref, dw_ref, dwb_ref, wd_ref,
                       o_ref, gn_ref, h_ref, acc_ref):
    f = pl.program_id(0)
    n_state, n = st_ref.shape[0], st_ref.shape[1]
    t_rows = x_ref.shape[0]

    @pl.when(f == 0)
    def _():
        h_ref[...] = _rms(x_ref[...], gpre_ref[...]).astype(BF16)
        acc_ref[...] = jnp.zeros_like(acc_ref)

    h = h_ref[...]
    g = _dot(h, wg_ref[...])
    u = _dot(h, wu_ref[...])
    s0, s1 = st_ref[0], st_ref[1]
    g1 = jnp.concatenate([s1, g[:t_rows - n]], axis=0)
    g2 = jnp.concatenate([s0, s1, g[:t_rows - 2 * n]], axis=0)
    gc = g2 * dw_ref[0:1, :] + g1 * dw_ref[1:2, :] + g * dw_ref[2:3, :] + dwb_ref[...]
    act = (_silu(gc) * u).astype(BF16)
    acc_ref[...] += _dot(act, wd_ref[...])
    for s in range(n_state):
        gn_ref[s] = g[t_rows - (n_state - s) * n:t_rows - (n_state - s - 1) * n]

    @pl.when(f == pl.num_programs(0) - 1)
    def _():
        _ffn_finish(x_ref, gpost_ref, acc_ref, o_ref)


def ffn_sample(x, g_pre, g_post, state, w_gate, w_up, dw, dw_b, w_down, *, tf):
    t, d = x.shape
    dff = w_gate.shape[1]
    n_state, n, _ = state.shape
    assert dw.shape[0] == 3 and n_state == 2 and t >= 2 * n
    return pl.pallas_call(
        _ffn_sample_kernel,
        grid=(dff // tf,),
        in_specs=[
            pl.BlockSpec((t, d), lambda f: (0, 0)),
            pl.BlockSpec((1, d), lambda f: (0, 0)),
            pl.BlockSpec((1, d), lambda f: (0, 0)),
            pl.BlockSpec((n_state, n, tf), lambda f: (0, 0, f)),
            pl.BlockSpec((d, tf), lambda f: (0, f)),
            pl.BlockSpec((d, tf), lambda f: (0, f)),
            pl.BlockSpec((3, tf), lambda f: (0, f)),
            pl.BlockSpec((1, tf), lambda f: (0, f)),
            pl.BlockSpec((tf, d), lambda f: (f, 0)),
        ],
        out_specs=[
            pl.BlockSpec((t, d), lambda f: (0, 0)),
            pl.BlockSpec((n_state, n, tf), lambda f: (0, 0, f)),
        ],
        out_shape=[jax.ShapeDtypeStruct((t, d), F32), jax.ShapeDtypeStruct(state.shape, F32)],
        scratch_shapes=[pltpu.VMEM((t, d), BF16), pltpu.VMEM((t, d), F32)],
        compiler_params=_params("arbitrary"),
        name="ffn_sample",
    )(x, g_pre, g_post, state, w_gate, w_up, dw, dw_b, w_down)


def _mla_in_kernel(x_ref, g_ref, w_ref, qn_ref, kvn_ref, cos_ref, sin_ref, qa_ref, ckv_ref, kr_ref,
                   *, q_lora, kv_lora, rope):
    h = _rms(x_ref[...], g_ref[...]).astype(BF16)
    z = _dot(h, w_ref[...])
    qa_ref[...] = _rms(z[:, :q_lora], qn_ref[...]).astype(BF16)
    ckv_ref[...] = _rms(z[:, q_lora:q_lora + kv_lora], kvn_ref[...])
    o = q_lora + kv_lora
    kr_ref[...] = z[:, o:o + rope] * cos_ref[...] + z[:, o + rope:o + 2 * rope] * sin_ref[...]


def mla_in(x, g_pre, w_ext, q_norm, kv_norm, cos, sin, *, tm):
    t, d = x.shape
    q_lora, kv_lora, rope = q_norm.shape[1], kv_norm.shape[1], cos.shape[1]
    n_ext = w_ext.shape[1]
    row = lambda i: (i, 0)
    fixed = lambda i: (0, 0)
    kern = functools.partial(_mla_in_kernel, q_lora=q_lora, kv_lora=kv_lora, rope=rope)
    return pl.pallas_call(
        kern,
        grid=(t // tm,),
        in_specs=[
            pl.BlockSpec((tm, d), row),
            pl.BlockSpec((1, d), fixed),
            pl.BlockSpec((d, n_ext), fixed),
            pl.BlockSpec((1, q_lora), fixed),
            pl.BlockSpec((1, kv_lora), fixed),
            pl.BlockSpec((tm, rope), row),
            pl.BlockSpec((tm, rope), row),
        ],
        out_specs=[pl.BlockSpec((tm, q_lora), row), pl.BlockSpec((tm, kv_lora), row),
                   pl.BlockSpec((tm, rope), row)],
        out_shape=[jax.ShapeDtypeStruct((t, q_lora), BF16), jax.ShapeDtypeStruct((t, kv_lora), F32),
                   jax.ShapeDtypeStruct((t, rope), F32)],
        compiler_params=_params("parallel"),
        name="mla_in",
    )(x, g_pre, w_ext, q_norm, kv_norm, cos, sin)


def _mla_qkv_kernel(qa_ref, ckv_ref, kr_ref, cos_ref, sin_ref, wq_ref, wuk_ref, wuv_ref,
                    q_ref, k_ref, v_ref, *, nope, rope, scale):
    q = _dot(qa_ref[...], wq_ref[0])
    qn = q[:, :nope] * scale
    qr = (q[:, nope:nope + rope] * cos_ref[...] + q[:, nope + rope:] * sin_ref[...]) * scale
    q_ref[0, 0] = jnp.concatenate([qn, qr], axis=1).astype(BF16)
    ckv = ckv_ref[...].astype(BF16)
    kn = _dot(ckv, wuk_ref[0])
    k_ref[0, 0] = jnp.concatenate([kn, kr_ref[...]], axis=1).astype(BF16)
    v_ref[0, 0] = _dot(ckv, wuv_ref[0]).astype(BF16)


def mla_qkv(qa, ckv, kr, cos, sin, wq, wuk, wuv, *, batch, seq, tm, scale):
    t, q_lora = qa.shape
    kv_lora, rope = ckv.shape[1], kr.shape[1]
    heads, _, nope = wuk.shape
    vdim = wuv.shape[2]
    tps = seq // tm
    row = lambda i, h: (i, 0)
    head = lambda i, h: (h, 0, 0)
    out = lambda i, h: (i // tps, h, i % tps, 0)
    kern = functools.partial(_mla_qkv_kernel, nope=nope, rope=rope, scale=scale)
    return pl.pallas_call(
        kern,
        grid=(t // tm, heads),
        in_specs=[
            pl.BlockSpec((tm, q_lora), row),
            pl.BlockSpec((tm, kv_lora), row),
            pl.BlockSpec((tm, rope), row),
            pl.BlockSpec((tm, rope), row),
            pl.BlockSpec((tm, rope), row),
            pl.BlockSpec((1, q_lora, nope + 2 * rope), head),
            pl.BlockSpec((1, kv_lora, nope), head),
            pl.BlockSpec((1, kv_lora, vdim), head),
        ],
        out_specs=[pl.BlockSpec((1, 1, tm, nope + rope), out), pl.BlockSpec((1, 1, tm, nope + rope), out),
                   pl.BlockSpec((1, 1, tm, vdim), out)],
        out_shape=[jax.ShapeDtypeStruct((batch, heads, seq, nope + rope), BF16),
                   jax.ShapeDtypeStruct((batch, heads, seq, nope + rope), BF16),
                   jax.ShapeDtypeStruct((batch, heads, seq, vdim), BF16)],
        compiler_params=_params("parallel", "arbitrary"),
        name="mla_qkv",
    )(qa, ckv, kr, cos, sin, wq, wuk, wuv)


def _flash_kernel(q_ref, k_ref, v_ref, o_ref, *, tk):
    tq = q_ref.shape[2]
    vdim = v_ref.shape[3]
    qi = pl.program_id(2)
    q = q_ref[0, 0]

    def step(kb, carry, masked):
        m, l, acc = carry
        rows = pl.ds(pl.multiple_of(kb * tk, tk), tk)
        s = _dot_nt(q, k_ref[0, 0, rows, :])
        if masked:
            r = lax.broadcasted_iota(jnp.int32, s.shape, 0)
            c = lax.broadcasted_iota(jnp.int32, s.shape, 1)
            s = jnp.where(c <= r, s, -jnp.inf)
        m_new = jnp.maximum(m, jnp.max(s, axis=-1, keepdims=True))
        alpha = jnp.exp(m - m_new)
        p = jnp.exp(s - m_new)
        l = alpha * l + jnp.sum(p, axis=-1, keepdims=True)
        acc = alpha * acc + _dot(p.astype(BF16), v_ref[0, 0, rows, :])
        return m_new, l, acc

    init = (jnp.full((tq, 1), -jnp.inf, F32), jnp.zeros((tq, 1), F32), jnp.zeros((tq, vdim), F32))
    carry = lax.fori_loop(0, qi, lambda kb, c: step(kb, c, False), init)
    m, l, acc = step(qi, carry, True)
    o_ref[0] = (acc / l).astype(o_ref.dtype)


def flash_prompt(q, k, v, *, tq):
    batch, heads, seq, dk = q.shape
    vdim = v.shape[3]
    kern = functools.partial(_flash_kernel, tk=tq)
    return pl.pallas_call(
        kern,
        grid=(batch, heads, seq // tq),
        in_specs=[
            pl.BlockSpec((1, 1, tq, dk), lambda n, h, i: (n, h, i, 0)),
            pl.BlockSpec((1, 1, seq, dk), lambda n, h, i: (n, h, 0, 0)),
            pl.BlockSpec((1, 1, seq, vdim), lambda n, h, i: (n, h, 0, 0)),
        ],
        out_specs=pl.BlockSpec((1, tq, vdim), lambda n, h, i: (n, i, h)),
        out_shape=jax.ShapeDtypeStruct((batch, seq, heads * vdim), BF16),
        compiler_params=_params("parallel", "parallel", "arbitrary"),
        name="flash_prompt",
    )(q, k, v)


def _proj_post_kernel(a_ref, x_ref, w_ref, gp_ref, o_ref):
    o_ref[...] = x_ref[...] + _rms(_dot(a_ref[...], w_ref[...]), gp_ref[...])


def proj_post(a, x, w, g_post, *, tm):
    t, kdim = a.shape
    d = x.shape[1]
    row = lambda i: (i, 0)
    fixed = lambda i: (0, 0)
    return pl.pallas_call(
        _proj_post_kernel,
        grid=(t // tm,),
        in_specs=[pl.BlockSpec((tm, kdim), row), pl.BlockSpec((tm, d), row), pl.BlockSpec((kdim, d), fixed),
                  pl.BlockSpec((1, d), fixed)],
        out_specs=pl.BlockSpec((tm, d), row),
        out_shape=jax.ShapeDtypeStruct((t, d), F32),
        compiler_params=_params("parallel"),
        name="proj_post",
    )(a, x, w, g_post)


def _sample_q_kernel(qa_ref, cos_ref, sin_ref, wq_ref, wukt_ref, o_ref, *, nope, rope, scale):
    q = _dot(qa_ref[...], wq_ref[0])
    qn = q[:, :nope].astype(BF16)
    qr = (q[:, nope:nope + rope] * cos_ref[...] + q[:, nope + rope:] * sin_ref[...]) * scale
    qlat = _dot(qn, wukt_ref[0]) * scale
    o_ref[0] = jnp.concatenate([qlat, qr], axis=1).astype(BF16)


def sample_q(qa, cos, sin, wq, wukt, *, scale):
    t, q_lora = qa.shape
    rope = cos.shape[1]
    heads, nope, kv_lora = wukt.shape
    fixed = lambda h: (0, 0)
    head = lambda h: (h, 0, 0)
    kern = functools.partial(_sample_q_kernel, nope=nope, rope=rope, scale=scale)
    return pl.pallas_call(
        kern,
        grid=(heads,),
        in_specs=[pl.BlockSpec((t, q_lora), fixed), pl.BlockSpec((t, rope), fixed),
                  pl.BlockSpec((t, rope), fixed),
                  pl.BlockSpec((1, q_lora, nope + 2 * rope), head), pl.BlockSpec((1, nope, kv_lora), head)],
        out_specs=pl.BlockSpec((1, t, kv_lora + rope), head),
        out_shape=jax.ShapeDtypeStruct((heads, t, kv_lora + rope), BF16),
        compiler_params=_params("parallel"),
        name="sample_q",
    )(qa, cos, sin, wq, wukt)


def _decode_kernel(pt_ref, q_ref, cn_ref, rn_ref, *rest, pages, heads):
    c_refs = rest[:pages]
    r_refs = rest[pages:2 * pages]
    o_ref, kc_ref, kr_ref, m_ref, l_ref, acc_ref = rest[2 * pages:]
    del pt_ref
    ci = pl.program_id(1)
    page = c_refs[0].shape[1]
    kv_lora = c_refs[0].shape[2]

    @pl.when(ci == 0)
    def _():
        m_ref[...] = jnp.full(m_ref.shape, -jnp.inf, F32)
        l_ref[...] = jnp.zeros(l_ref.shape, F32)
        acc_ref[...] = jnp.zeros(acc_ref.shape, F32)

    for j in range(pages):
        kc_ref[j * page:(j + 1) * page, :] = c_refs[j][0].astype(BF16)
        kr_ref[j * page:(j + 1) * page, :] = r_refs[j][0].astype(BF16)
    q = q_ref[0]
    qlat, qr = q[:, :kv_lora], q[:, kv_lora:]
    kc = kc_ref[...]
    s = _dot_nt(qlat, kc) + _dot_nt(qr, kr_ref[...])
    m = m_ref[...]
    m_new = jnp.maximum(m, jnp.max(s, axis=-1, keepdims=True))
    alpha = jnp.exp(m - m_new)
    p = jnp.exp(s - m_new)
    l_ref[...] = alpha * l_ref[...] + jnp.sum(p, axis=-1, keepdims=True)
    acc_ref[...] = alpha * acc_ref[...] + _dot(p.astype(BF16), kc)
    m_ref[...] = m_new

    @pl.when(ci == pl.num_programs(1) - 1)
    def _():
        n_new = cn_ref.shape[1]
        cn = cn_ref[0].astype(BF16).astype(F32)
        rn = rn_ref[0].astype(BF16).astype(F32)
        qlf, qrf = qlat.astype(F32), qr.astype(F32)
        row = lax.broadcasted_iota(jnp.int32, (q.shape[0], 1), 0)
        cols = []
        for j in range(n_new):
            sj = (jnp.sum(qlf * cn[j:j + 1], axis=-1, keepdims=True)
                  + jnp.sum(qrf * rn[j:j + 1], axis=-1, keepdims=True))
            cols.append(jnp.where(row >= j * heads, sj, -jnp.inf))
        m0 = m_ref[...]
        m1 = m0
        for sj in cols:
            m1 = jnp.maximum(m1, sj)
        a = jnp.exp(m0 - m1)
        l1 = a * l_ref[...]
        acc1 = a * acc_ref[...]
        for j, sj in enumerate(cols):
            pj = jnp.exp(sj - m1)
            l1 = l1 + pj
            acc1 = acc1 + pj.astype(BF16).astype(F32) * cn[j:j + 1]
        o_ref[0] = acc1 / l1


def decode_attention(page_table, qs, c_new, r_new, cache_c, cache_r, *, pages, heads):
    n, rows, dq = qs.shape
    n_pages = page_table.shape[1]
    _, page, kv_lora = cache_c.shape
    rope = cache_r.shape[2]
    n_new = c_new.shape[1]
    assert n_pages % pages == 0
    pt = page_table.reshape(-1)

    def page_map(j):
        return lambda i, c, pt_ref: (pt_ref[i * n_pages + c * pages + j], 0, 0)

    seq = lambda i, c, pt_ref: (i, 0, 0)
    grid_spec = pltpu.PrefetchScalarGridSpec(
        num_scalar_prefetch=1,
        grid=(n, n_pages // pages),
        in_specs=[pl.BlockSpec((1, rows, dq), seq), pl.BlockSpec((1, n_new, kv_lora), seq),
                  pl.BlockSpec((1, n_new, rope), seq)]
        + [pl.BlockSpec((1, page, kv_lora), page_map(j)) for j in range(pages)]
        + [pl.BlockSpec((1, page, rope), page_map(j)) for j in range(pages)],
        out_specs=pl.BlockSpec((1, rows, kv_lora), seq),
        scratch_shapes=[pltpu.VMEM((pages * page, kv_lora), BF16), pltpu.VMEM((pages * page, rope), BF16),
                        pltpu.VMEM((rows, 1), F32), pltpu.VMEM((rows, 1), F32),
                        pltpu.VMEM((rows, kv_lora), F32)],
    )
    return pl.pallas_call(
        functools.partial(_decode_kernel, pages=pages, heads=heads),
        grid_spec=grid_spec,
        out_shape=jax.ShapeDtypeStruct((n, rows, kv_lora), F32),
        compiler_params=_params("parallel", "arbitrary"),
        name="decode_attention",
    )(pt, qs, c_new, r_new, *([cache_c] * pages), *([cache_r] * pages))


def _sample_uv_kernel(ol_ref, wuv_ref, o_ref):
    o_ref[...] = _dot(ol_ref[0].astype(BF16), wuv_ref[0]).astype(o_ref.dtype)


def sample_uv(o_lat, wuv):
    heads, t, kv_lora = o_lat.shape
    vdim = wuv.shape[2]
    head = lambda h: (h, 0, 0)
    return pl.pallas_call(
        _sample_uv_kernel,
        grid=(heads,),
        in_specs=[pl.BlockSpec((1, t, kv_lora), head), pl.BlockSpec((1, kv_lora, vdim), head)],
        out_specs=pl.BlockSpec((t, vdim), lambda h: (0, h)),
        out_shape=jax.ShapeDtypeStruct((t, heads * vdim), BF16),
        compiler_params=_params("parallel"),
        name="sample_uv",
    )(o_lat, wuv)


def _rot_cols(w, half):
    return jnp.concatenate([-w[..., half:], w[..., :half]], axis=-1)


def _rope_tables(pos, rope):
    inv_freq = ROPE_BASE ** (-jnp.arange(0, rope, 2, dtype=F32) / rope)
    ang = pos.astype(F32)[:, None] * inv_freq[None, :]
    cos, sin = jnp.cos(ang), jnp.sin(ang)
    return jnp.concatenate([cos, cos], axis=-1), jnp.concatenate([sin, sin], axis=-1)


def kernel(x_prompt, x_sample, cache_kv_latent, cache_k_rope, state_conv, state_ffn_conv, page_table, norm_mix_pre, norm_mix_post, norm_ffn_pre, norm_ffn_post, cv_w_in, cv_b_in, cv_dw, cv_dw_b, cv_ln_g, cv_ln_b, cv_w_out, cv_b_out, mla_w_in, mla_q_norm, mla_kv_norm, mla_w_qb, mla_w_uk, mla_w_uv, mla_w_o, ffn_w_gate, ffn_w_up, ffn_dw, ffn_dw_b, ffn_w_down):
    batch, seq, d = x_prompt.shape
    n_dec, l_dec, _ = x_sample.shape
    depth = norm_mix_pre.shape[0]
    conv_w = cv_dw.shape[1]
    ffn_w = ffn_dw.shape[1]
    heads, nope = mla_w_uk.shape[2], mla_w_uk.shape[3]
    q_lora, kv_lora = mla_q_norm.shape[1], mla_kv_norm.shape[1]
    rope = mla_w_in.shape[2] - q_lora - kv_lora
    vdim = mla_w_uv.shape[3]
    dff = ffn_w_gate.shape[2]
    past = page_table.shape[1] * cache_kv_latent.shape[2]
    scale = 1.0 / math.sqrt(nope + rope)
    tp = batch * seq
    ts = n_dec * l_dec

    tm = min(512, seq)
    tf = 512 if dff % 512 == 0 else dff
    tn = min(512, d)
    tm_conv = min(256, seq)
    pages = min(16, page_table.shape[1])

    vec = lambda a: a[:, None, :]
    n_pre, n_post, f_pre, f_post = vec(norm_mix_pre), vec(norm_mix_post), vec(norm_ffn_pre), vec(norm_ffn_post)

    cv_w_in_b, cv_w_out_b = cv_w_in.astype(BF16), cv_w_out.astype(BF16)
    w_gate_b, w_up_b, w_down_b = ffn_w_gate.astype(BF16), ffn_w_up.astype(BF16), ffn_w_down.astype(BF16)
    w_in_ext = jnp.concatenate([mla_w_in, _rot_cols(mla_w_in[..., q_lora + kv_lora:], rope // 2)],
                               axis=-1).astype(BF16)
    wq_rope = mla_w_qb[..., nope:]
    wq = jnp.concatenate([mla_w_qb, _rot_cols(wq_rope, rope // 2)], axis=-1)
    wq = wq.transpose(0, 2, 1, 3).astype(BF16)
    wuk = mla_w_uk.transpose(0, 2, 1, 3).astype(BF16)
    wukt = mla_w_uk.transpose(0, 2, 3, 1).astype(BF16)
    wuv = mla_w_uv.transpose(0, 2, 1, 3).astype(BF16)
    w_o = mla_w_o.reshape(mla_w_o.shape[0], heads * vdim, d).astype(BF16)

    cos_p, sin_p = _rope_tables(jnp.tile(jnp.arange(seq), batch), rope)
    cos_s, sin_s = _rope_tables(jnp.repeat(past + jnp.arange(l_dec), n_dec), rope)

    xp = x_prompt.reshape(tp, d)
    kv_p, kr_p, conv_p, ffn_p = [], [], [], []
    xs = x_sample.transpose(1, 0, 2).reshape(ts, d)
    kv_s, kr_s, conv_s, ffn_s = [], [], [], []

    for i in range(depth):
        j = i // 2
        if i % 2 == 0:
            b_in = cv_b_in[j][None, :]
            args = (cv_dw[j], cv_dw_b[j][None, :], cv_ln_g[j][None, :], cv_ln_b[j][None, :], cv_w_out_b[j],
                    cv_b_out[j][None, :], n_post[i])
            u = conv_in(xp, n_pre[i], cv_w_in_b[j], b_in, tm=tm, tn=tn)
            conv_p.append(u.reshape(batch, seq, d)[:, seq - (conv_w - 1):])
            xp = conv_out_prompt(u, xp, *args, seq=seq, tm=tm_conv)

            us = conv_in(xs, n_pre[i], cv_w_in_b[j], b_in, tm=ts, tn=tn)
            u_ext = jnp.concatenate([state_conv[j].transpose(1, 0, 2), us.reshape(l_dec, n_dec, d)], axis=0)
            conv_s.append(u_ext[l_dec:].transpose(1, 0, 2))
            xs = conv_out_sample(u_ext, xs.reshape(l_dec, n_dec, d), *args, ns=32).reshape(ts, d)
        else:
            w = (w_in_ext[j], mla_q_norm[j][None, :], mla_kv_norm[j][None, :])
            qa, ckv, kr = mla_in(xp, n_pre[i], *w, cos_p, sin_p, tm=tm)
            kv_p.append(ckv.reshape(batch, seq, kv_lora))
            kr_p.append(kr.reshape(batch, seq, rope))
            q, k, v = mla_qkv(qa, ckv, kr, cos_p, sin_p, wq[j], wuk[j], wuv[j], batch=batch, seq=seq, tm=tm,
                              scale=scale)
            o = flash_prompt(q, k, v, tq=tm)
            xp = proj_post(o.reshape(tp, heads * vdim), xp, w_o[j], n_post[i], tm=tm)

            qa, ckv, kr = mla_in(xs, n_pre[i], *w, cos_s, sin_s, tm=ts)
            c_new = ckv.reshape(l_dec, n_dec, kv_lora).transpose(1, 0, 2)
            r_new = kr.reshape(l_dec, n_dec, rope).transpose(1, 0, 2)
            kv_s.append(c_new)
            kr_s.append(r_new)
            qs = sample_q(qa, cos_s, sin_s, wq[j], wukt[j], scale=scale)
            qs = qs.reshape(heads, l_dec, n_dec, kv_lora + rope).transpose(2, 1, 0, 3)
            qs = qs.reshape(n_dec, l_dec * heads, kv_lora + rope)
            o_lat = decode_attention(page_table, qs, c_new, r_new, cache_kv_latent[j], cache_k_rope[j],
                                     pages=pages, heads=heads)
            o_lat = o_lat.reshape(n_dec, l_dec, heads, kv_lora).transpose(2, 1, 0, 3).reshape(heads, ts, kv_lora)
            xs = proj_post(sample_uv(o_lat, wuv[j]), xs, w_o[j], n_post[i], tm=ts)

        ffn_w_args = (w_gate_b[i], w_up_b[i], ffn_dw[i], ffn_dw_b[i][None, :], w_down_b[i])
        xp, g_tail = ffn_prompt(xp, f_pre[i], f_post[i], *ffn_w_args, seq=seq, tm=tm, tf=tf)
        g_tail = g_tail.reshape(batch, seq // tm, FFN_TAIL, dff)
        ffn_p.append(g_tail[:, -1, FFN_TAIL - (ffn_w - 1):])
        xs, g_new = ffn_sample(xs, f_pre[i], f_post[i], state_ffn_conv[i].transpose(1, 0, 2), *ffn_w_args, tf=tf)
        ffn_s.append(g_new.transpose(1, 0, 2))

    y_prompt = xp.reshape(batch, seq, d)
    y_sample = xs.reshape(l_dec, n_dec, d).transpose(1, 0, 2)
    return (y_prompt, y_sample, jnp.stack(kv_p), jnp.stack(kr_p), jnp.stack(kv_s), jnp.stack(kr_s),
            jnp.stack(conv_p), jnp.stack(conv_s), jnp.stack(ffn_p), jnp.stack(ffn_s))
```

```python
import functools
import math

import jax
import jax.numpy as jnp
from jax import lax
from jax.experimental import pallas as pl
from jax.experimental.pallas import tpu as pltpu

EPS = 1e-6
ROPE_BASE = 10000.0
F32 = jnp.float32
BF16 = jnp.bfloat16

V7X_VMEM_LIMIT_BYTES = 56 * 1024 * 1024


def _params(*sem):
    return pltpu.CompilerParams(dimension_semantics=sem, vmem_limit_bytes=V7X_VMEM_LIMIT_BYTES)


def _dot(a, b):
    return jnp.dot(a, b, preferred_element_type=F32)


def _dot_nt(a, b):
    return lax.dot_general(a, b, (((1,), (1,)), ((), ())), preferred_element_type=F32)


def _rms(xf, g):
    ms = jnp.mean(xf * xf, axis=-1, keepdims=True)
    return (xf * lax.rsqrt(ms + EPS)) * g


def _silu(x):
    return x * jax.nn.sigmoid(x)


def _conv_in_kernel(x_ref, g_ref, w1_ref, w2_ref, b1_ref, b2_ref, u_ref, h_ref):
    @pl.when(pl.program_id(1) == 0)
    def _():
        h_ref[...] = _rms(x_ref[...], g_ref[...]).astype(BF16)

    h = h_ref[...]
    a1 = _dot(h, w1_ref[...]) + b1_ref[...]
    a2 = _dot(h, w2_ref[...]) + b2_ref[...]
    u_ref[...] = a1 * jax.nn.sigmoid(a2)


def conv_in(x, g_pre, w_in, b_in, *, tm, tn):
    t, d = x.shape
    nj = d // tn
    return pl.pallas_call(
        _conv_in_kernel,
        grid=(t // tm, nj),
        in_specs=[
            pl.BlockSpec((tm, d), lambda i, j: (i, 0)),
            pl.BlockSpec((1, d), lambda i, j: (0, 0)),
            pl.BlockSpec((d, tn), lambda i, j: (0, j)),
            pl.BlockSpec((d, tn), lambda i, j: (0, j + nj)),
            pl.BlockSpec((1, tn), lambda i, j: (0, j)),
            pl.BlockSpec((1, tn), lambda i, j: (0, j + nj)),
        ],
        out_specs=pl.BlockSpec((tm, tn), lambda i, j: (i, j)),
        out_shape=jax.ShapeDtypeStruct((t, d), F32),
        scratch_shapes=[pltpu.VMEM((tm, d), BF16)],
        compiler_params=_params("parallel", "arbitrary"),
        name="conv_in",
    )(x, g_pre, w_in, w_in, b_in, b_in)


SUBLANES = 8
CONV_HALO = 32


def _conv_out_prompt_kernel(u_ref, uh_ref, x_ref, dw_ref, dwb_ref, lng_ref, lnb_ref, w_ref, b_ref, gp_ref,
                            o_ref, ext_ref, c_ref, sh_ref, *, tiles_per_seq, width, lane_chunk):
    tm, d = u_ref.shape
    first = pl.program_id(0) % tiles_per_seq == 0
    ext_ref[0:CONV_HALO, :] = jnp.where(first, 0.0, uh_ref[...])
    ext_ref[CONV_HALO:, :] = u_ref[...]
    base = CONV_HALO - (width - 1)
    n_sh = sh_ref.shape[1]

    def chunk(c, carry):
        col = pl.ds(pl.multiple_of(c * lane_chunk, lane_chunk), lane_chunk)
        for b in range(1, SUBLANES):
            sh_ref[b - 1] = ext_ref[pl.ds(b, n_sh), col]
        acc = jnp.zeros((tm, lane_chunk), F32)
        for k in range(width):
            b = (base + k) % SUBLANES
            a8 = base + k - b
            tap = ext_ref[pl.ds(a8, tm), col] if b == 0 else sh_ref[b - 1, pl.ds(a8, tm), :]
            acc = acc + tap * dw_ref[pl.ds(k, 1), col]
        c_ref[:, col] = acc + dwb_ref[:, col]
        return carry

    lax.fori_loop(0, d // lane_chunk, chunk, 0)
    c = c_ref[...]
    mu = jnp.mean(c, axis=-1, keepdims=True)
    xc = c - mu
    var = jnp.mean(xc * xc, axis=-1, keepdims=True)
    y = xc * lax.rsqrt(var + EPS) * lng_ref[...] + lnb_ref[...]
    z = _dot(_silu(y).astype(BF16), w_ref[...]) + b_ref[...]
    o_ref[...] = x_ref[...] + _rms(z, gp_ref[...])


def conv_out_prompt(u, x, dw, dw_b, ln_g, ln_b, w_out, b_out, g_post, *, seq, tm):
    t, d = u.shape
    width = dw.shape[0]
    hb = tm // CONV_HALO
    row = lambda i: (i, 0)
    fixed = lambda i: (0, 0)
    kern = functools.partial(_conv_out_prompt_kernel, tiles_per_seq=seq // tm, width=width, lane_chunk=128)
    return pl.pallas_call(
        kern,
        grid=(t // tm,),
        in_specs=[
            pl.BlockSpec((tm, d), row),
            pl.BlockSpec((CONV_HALO, d), lambda i: (jnp.maximum(i * hb - 1, 0), 0)),
            pl.BlockSpec((tm, d), row),
            pl.BlockSpec((width, d), fixed),
            pl.BlockSpec((1, d), fixed),
            pl.BlockSpec((1, d), fixed),
            pl.BlockSpec((1, d), fixed),
            pl.BlockSpec((d, d), fixed),
            pl.BlockSpec((1, d), fixed),
            pl.BlockSpec((1, d), fixed),
        ],
        out_specs=pl.BlockSpec((tm, d), row),
        out_shape=jax.ShapeDtypeStruct((t, d), F32),
        scratch_shapes=[pltpu.VMEM((tm + CONV_HALO, d), F32), pltpu.VMEM((tm, d), F32),
                        pltpu.VMEM((SUBLANES - 1, tm + CONV_HALO - SUBLANES, 128), F32)],
        compiler_params=_params("parallel"),
        name="conv_out_prompt",
    )(u, u, x, dw, dw_b, ln_g, ln_b, w_out, b_out, g_post)


def _conv_out_sample_kernel(ue_ref, x_ref, dw_ref, dwb_ref, lng_ref, lnb_ref, w_ref, b_ref, gp_ref, o_ref,
                            *, width):
    n_tok, ns, d = x_ref.shape
    acts = []
    for t in range(n_tok):
        c = jnp.zeros((ns, d), F32)
        for k in range(width):
            c = c + ue_ref[t + k] * dw_ref[pl.ds(k, 1), :]
        c = c + dwb_ref[...]
        mu = jnp.mean(c, axis=-1, keepdims=True)
        xc = c - mu
        var = jnp.mean(xc * xc, axis=-1, keepdims=True)
        y = xc * lax.rsqrt(var + EPS) * lng_ref[...] + lnb_ref[...]
        acts.append(_silu(y).astype(BF16))
    z = _dot(jnp.concatenate(acts, axis=0), w_ref[...]) + b_ref[...]
    r = _rms(z, gp_ref[...])
    for t in range(n_tok):
        o_ref[t] = x_ref[t] + r[t * ns:(t + 1) * ns]


def conv_out_sample(u_ext, x, dw, dw_b, ln_g, ln_b, w_out, b_out, g_post, *, ns):
    rows, n, d = u_ext.shape
    n_tok = x.shape[0]
    width = dw.shape[0]
    fixed = lambda i: (0, 0)
    return pl.pallas_call(
        functools.partial(_conv_out_sample_kernel, width=width),
        grid=(n // ns,),
        in_specs=[
            pl.BlockSpec((rows, ns, d), lambda i: (0, i, 0)),
            pl.BlockSpec((n_tok, ns, d), lambda i: (0, i, 0)),
            pl.BlockSpec((width, d), fixed),
            pl.BlockSpec((1, d), fixed),
            pl.BlockSpec((1, d), fixed),
            pl.BlockSpec((1, d), fixed),
            pl.BlockSpec((d, d), fixed),
            pl.BlockSpec((1, d), fixed),
            pl.BlockSpec((1, d), fixed),
        ],
        out_specs=pl.BlockSpec((n_tok, ns, d), lambda i: (0, i, 0)),
        out_shape=jax.ShapeDtypeStruct(x.shape, F32),
        compiler_params=_params("parallel"),
        name="conv_out_sample",
    )(u_ext, x, dw, dw_b, ln_g, ln_b, w_out, b_out, g_post)


FFN_TAIL = 8


def _ffn_finish(x_ref, gpost_ref, acc_ref, o_ref):
    o_ref[...] = x_ref[...] + _rms(acc_ref[...], gpost_ref[...])


def _ffn_prompt_kernel(x_ref, gpre_ref, gpost_ref, wg_ref, wu_ref, dw_ref, dwb_ref, wd_ref,
                       o_ref, gt_ref, h_ref, carry_ref, *, tiles_per_seq):
    acc_ref = o_ref
    i, f = pl.program_id(0), pl.program_id(1)
    tm = x_ref.shape[0]

    @pl.when(f == 0)
    def _():
        h_ref[...] = _rms(x_ref[...], gpre_ref[...]).astype(BF16)
        acc_ref[...] = jnp.zeros_like(acc_ref)

    @pl.when(i % tiles_per_seq == 0)
    def _():
        carry_ref[f] = jnp.zeros(carry_ref.shape[1:], F32)

    h = h_ref[...]
    g = _dot(h, wg_ref[...])
    u = _dot(h, wu_ref[...])
    prev = carry_ref[f]
    p1 = prev[FFN_TAIL - 1:FFN_TAIL]
    p2 = prev[FFN_TAIL - 2:FFN_TAIL - 1]
    row = lax.broadcasted_iota(jnp.int32, g.shape, 0)
    g1 = jnp.where(row == 0, p1, pltpu.roll(g, 1, 0))
    g2 = jnp.where(row == 0, p2, jnp.where(row == 1, p1, pltpu.roll(g, 2, 0)))
    gc = g2 * dw_ref[0:1, :] + g1 * dw_ref[1:2, :] + g * dw_ref[2:3, :] + dwb_ref[...]
    act = (_silu(gc) * u).astype(BF16)
    acc_ref[...] += _dot(act, wd_ref[...])
    tail = g[tm - FFN_TAIL:, :]
    carry_ref[f] = tail
    gt_ref[...] = tail

    @pl.when(f == pl.num_programs(1) - 1)
    def _():
        _ffn_finish(x_ref, gpost_ref, acc_ref, o_ref)


def ffn_prompt(x, g_pre, g_post, w_gate, w_up, dw, dw_b, w_down, *, seq, tm, tf):
    t, d = x.shape
    dff = w_gate.shape[1]
    nm, nf = t // tm, dff // tf
    kern = functools.partial(_ffn_prompt_kernel, tiles_per_seq=seq // tm)
    return pl.pallas_call(
        kern,
        grid=(nm, nf),
        in_specs=[
            pl.BlockSpec((tm, d), lambda i, f: (i, 0), pipeline_mode=pl.Buffered(1)),
            pl.BlockSpec((1, d), lambda i, f: (0, 0)),
            pl.BlockSpec((1, d), lambda i, f: (0, 0)),
            pl.BlockSpec((d, tf), lambda i, f: (0, f)),
            pl.BlockSpec((d, tf), lambda i, f: (0, f)),
            pl.BlockSpec((dw.shape[0], tf), lambda i, f: (0, f)),
            pl.BlockSpec((1, tf), lambda i, f: (0, f)),
            pl.BlockSpec((tf, d), lambda i, f: (f, 0)),
        ],
        out_specs=[
            pl.BlockSpec((tm, d), lambda i, f: (i, 0)),
            pl.BlockSpec((FFN_TAIL, tf), lambda i, f: (i, f)),
        ],
        out_shape=[jax.ShapeDtypeStruct((t, d), F32), jax.ShapeDtypeStruct((nm * FFN_TAIL, dff), F32)],
        scratch_shapes=[pltpu.VMEM((tm, d), BF16), pltpu.VMEM((nf, FFN_TAIL, tf), F32)],
        compiler_params=_params("arbitrary", "arbitrary"),
        name="ffn_prompt",
    )(x, g_pre, g_post, w_gate, w_up, dw, dw_b, w_down)


def _ffn_sample_kernel(x_ref, gpre_ref, gpost_ref, st_ref, wg_ref, wu_ref, dw_ref, dwb_ref, wd_ref,
                       o_ref, gn_ref, h_ref, acc_ref):
    f = pl.program_id(0)
    n_state, n = st_ref.shape[0], st_ref.shape[1]
    t_rows = x_ref.shape[0]

    @pl.when(f == 0)
    def _():
        h_ref[...] = _rms(x_ref[...], gpre_ref[...]).astype(BF16)
        acc_ref[...] = jnp.zeros_like(acc_ref)

    h = h_ref[...]
    g = _dot(h, wg_ref[...])
    u = _dot(h, wu_ref[...])
    s0, s1 = st_ref[0], st_ref[1]
    g1 = jnp.concatenate([s1, g[:t_rows - n]], axis=0)
    g2 = jnp.concatenate([s0, s1, g[:t_rows - 2 * n]], axis=0)
    gc = g2 * dw_ref[0:1, :] + g1 * dw_ref[1:2, :] + g * dw_ref[2:3, :] + dwb_ref[...]
    act = (_silu(gc) * u).astype(BF16)
    acc_ref[...] += _dot(act, wd_ref[...])
    for s in range(n_state):
        gn_ref[s] = g[t_rows - (n_state - s) * n:t_rows - (n_state - s - 1) * n]

    @pl.when(f == pl.num_programs(0) - 1)
    def _():
        _ffn_finish(x_ref, gpost_ref, acc_ref, o_ref)


def ffn_sample(x, g_pre, g_post, state, w_gate, w_up, dw, dw_b, w_down, *, tf):
    t, d = x.shape
    dff = w_gate.shape[1]
    n_state, n, _ = state.shape
    assert dw.shape[0] == 3 and n_state == 2 and t >= 2 * n
    return pl.pallas_call(
        _ffn_sample_kernel,
        grid=(dff // tf,),
        in_specs=[
            pl.BlockSpec((t, d), lambda f: (0, 0)),
            pl.BlockSpec((1, d), lambda f: (0, 0)),
            pl.BlockSpec((1, d), lambda f: (0, 0)),
            pl.BlockSpec((n_state, n, tf), lambda f: (0, 0, f)),
            pl.BlockSpec((d, tf), lambda f: (0, f)),
            pl.BlockSpec((d, tf), lambda f: (0, f)),
            pl.BlockSpec((3, tf), lambda f: (0, f)),
            pl.BlockSpec((1, tf), lambda f: (0, f)),
            pl.BlockSpec((tf, d), lambda f: (f, 0)),
        ],
        out_specs=[
            pl.BlockSpec((t, d), lambda f: (0, 0)),
            pl.BlockSpec((n_state, n, tf), lambda f: (0, 0, f)),
        ],
        out_shape=[jax.ShapeDtypeStruct((t, d), F32), jax.ShapeDtypeStruct(state.shape, F32)],
        scratch_shapes=[pltpu.VMEM((t, d), BF16), pltpu.VMEM((t, d), F32)],
        compiler_params=_params("arbitrary"),
        name="ffn_sample",
    )(x, g_pre, g_post, state, w_gate, w_up, dw, dw_b, w_down)


def _mla_in_kernel(x_ref, g_ref, w_ref, qn_ref, kvn_ref, cos_ref, sin_ref, qa_ref, ckv_ref, kr_ref,
                   *, q_lora, kv_lora, rope):
    h = _rms(x_ref[...], g_ref[...]).astype(BF16)
    z = _dot(h, w_ref[...])
    qa_ref[...] = _rms(z[:, :q_lora], qn_ref[...]).astype(BF16)
    ckv_ref[...] = _rms(z[:, q_lora:q_lora + kv_lora], kvn_ref[...])
    o = q_lora + kv_lora
    kr_ref[...] = z[:, o:o + rope] * cos_ref[...] + z[:, o + rope:o + 2 * rope] * sin_ref[...]


def mla_in(x, g_pre, w_ext, q_norm, kv_norm, cos, sin, *, tm):
    t, d = x.shape
    q_lora, kv_lora, rope = q_norm.shape[1], kv_norm.shape[1], cos.shape[1]
    n_ext = w_ext.shape[1]
    row = lambda i: (i, 0)
    fixed = lambda i: (0, 0)
    kern = functools.partial(_mla_in_kernel, q_lora=q_lora, kv_lora=kv_lora, rope=rope)
    return pl.pallas_call(
        kern,
        grid=(t // tm,),
        in_specs=[
            pl.BlockSpec((tm, d), row),
            pl.BlockSpec((1, d), fixed),
            pl.BlockSpec((d, n_ext), fixed),
            pl.BlockSpec((1, q_lora), fixed),
            pl.BlockSpec((1, kv_lora), fixed),
            pl.BlockSpec((tm, rope), row),
            pl.BlockSpec((tm, rope), row),
        ],
        out_specs=[pl.BlockSpec((tm, q_lora), row), pl.BlockSpec((tm, kv_lora), row),
                   pl.BlockSpec((tm, rope), row)],
        out_shape=[jax.ShapeDtypeStruct((t, q_lora), BF16), jax.ShapeDtypeStruct((t, kv_lora), F32),
                   jax.ShapeDtypeStruct((t, rope), F32)],
        compiler_params=_params("parallel"),
        name="mla_in",
    )(x, g_pre, w_ext, q_norm, kv_norm, cos, sin)


def _mla_qkv_kernel(qa_ref, ckv_ref, kr_ref, cos_ref, sin_ref, wq_ref, wuk_ref, wuv_ref,
                    q_ref, k_ref, v_ref, *, heads, nope, rope, scale):
    tm = qa_ref.shape[0]
    hd = nope + 2 * rope
    qa = qa_ref[...]
    ckv = ckv_ref[...].astype(BF16)
    cos, sin = cos_ref[...], sin_ref[...]
    kr = kr_ref[...].astype(BF16)
    pad = jnp.zeros((tm, hd - nope - rope), BF16)
    for h in range(heads):
        q = _dot(qa, wq_ref[:, h * hd:(h + 1) * hd])
        qn = q[:, :nope] * scale
        qr = (q[:, nope:nope + rope] * cos + q[:, nope + rope:] * sin) * scale
        q_ref[:, h * hd:h * hd + nope] = qn.astype(BF16)
        q_ref[:, h * hd + nope:h * hd + nope + rope] = qr.astype(BF16)
        q_ref[:, h * hd + nope + rope:(h + 1) * hd] = pad
    for hp in range(heads // 2):
        kn2 = _dot(ckv, wuk_ref[:, 2 * hp * nope:2 * (hp + 1) * nope])
        for j in range(2):
            h = 2 * hp + j
            k_ref[:, h * hd:h * hd + nope] = kn2[:, j * nope:(j + 1) * nope].astype(BF16)
            k_ref[:, h * hd + nope:h * hd + nope + rope] = kr
            k_ref[:, h * hd + nope + rope:(h + 1) * hd] = pad
    v_ref[...] = _dot(ckv, wuv_ref[...]).astype(BF16)


def mla_qkv(qa, ckv, kr, cos, sin, wq, wuk, wuv, *, heads, tm, scale):
    t, q_lora = qa.shape
    kv_lora, rope = ckv.shape[1], kr.shape[1]
    nope = wuk.shape[1] // heads
    hd = nope + 2 * rope
    assert heads % 2 == 0 and wq.shape[1] == heads * hd
    row = lambda i: (i, 0)
    fixed = lambda i: (0, 0)
    kern = functools.partial(_mla_qkv_kernel, heads=heads, nope=nope, rope=rope, scale=scale)
    return pl.pallas_call(
        kern,
        grid=(t // tm,),
        in_specs=[
            pl.BlockSpec((tm, q_lora), row),
            pl.BlockSpec((tm, kv_lora), row),
            pl.BlockSpec((tm, rope), row),
            pl.BlockSpec((tm, rope), row),
            pl.BlockSpec((tm, rope), row),
            pl.BlockSpec(wq.shape, fixed),
            pl.BlockSpec(wuk.shape, fixed),
            pl.BlockSpec(wuv.shape, fixed),
        ],
        out_specs=[pl.BlockSpec((tm, heads * hd), row), pl.BlockSpec((tm, heads * hd), row),
                   pl.BlockSpec((tm, wuv.shape[1]), row)],
        out_shape=[jax.ShapeDtypeStruct((t, heads * hd), BF16), jax.ShapeDtypeStruct((t, heads * hd), BF16),
                   jax.ShapeDtypeStruct((t, wuv.shape[1]), BF16)],
        compiler_params=_params("parallel"),
        name="mla_qkv",
    )(qa, ckv, kr, cos, sin, wq, wuk, wuv)


def _flash_kernel(q_ref, k_ref, v_ref, o_ref, *, tk, hb):
    tq = q_ref.shape[1]
    hd = q_ref.shape[2] // hb
    vdim = v_ref.shape[2] // hb
    qi = pl.program_id(2)
    qs = [q_ref[0, :, h * hd:(h + 1) * hd] for h in range(hb)]

    def step(h, kb, carry, masked):
        m, l, acc = carry
        rows = pl.ds(pl.multiple_of(kb * tk, tk), tk)
        s = _dot_nt(qs[h], k_ref[0, rows, h * hd:(h + 1) * hd])
        if masked:
            r = lax.broadcasted_iota(jnp.int32, s.shape, 0)
            c = lax.broadcasted_iota(jnp.int32, s.shape, 1)
            s = jnp.where(c <= r, s, -jnp.inf)
        m_new = jnp.maximum(m, jnp.max(s, axis=-1, keepdims=True))
        alpha = jnp.exp(m - m_new)
        p = jnp.exp(s - m_new)
        l = alpha * l + jnp.sum(p, axis=-1, keepdims=True)
        acc = alpha * acc + _dot(p.astype(BF16), v_ref[0, rows, h * vdim:(h + 1) * vdim])
        return m_new, l, acc

    init1 = (jnp.full((tq, 1), -jnp.inf, F32), jnp.zeros((tq, 1), F32), jnp.zeros((tq, vdim), F32))
    body = lambda kb, cs: tuple(step(h, kb, cs[h], False) for h in range(hb))
    carries = lax.fori_loop(0, qi, body, tuple(init1 for _ in range(hb)))
    for h in range(hb):
        m, l, acc = step(h, qi, carries[h], True)
        o_ref[0, :, h * vdim:(h + 1) * vdim] = (acc / l).astype(o_ref.dtype)


def flash_prompt(q, k, v, *, heads, tq, hb):
    batch, seq, _ = q.shape
    hd = q.shape[2] // heads
    vdim = v.shape[2] // heads
    kern = functools.partial(_flash_kernel, tk=tq, hb=hb)
    return pl.pallas_call(
        kern,
        grid=(batch, heads // hb, seq // tq),
        in_specs=[
            pl.BlockSpec((1, tq, hb * hd), lambda n, g, i: (n, i, g)),
            pl.BlockSpec((1, seq, hb * hd), lambda n, g, i: (n, 0, g)),
            pl.BlockSpec((1, seq, hb * vdim), lambda n, g, i: (n, 0, g)),
        ],
        out_specs=pl.BlockSpec((1, tq, hb * vdim), lambda n, g, i: (n, i, g)),
        out_shape=jax.ShapeDtypeStruct((batch, seq, heads * vdim), BF16),
        compiler_params=_params("parallel", "parallel", "arbitrary"),
        name="flash_prompt",
    )(q, k, v)


def _proj_post_kernel(a_ref, x_ref, w_ref, gp_ref, o_ref):
    o_ref[...] = x_ref[...] + _rms(_dot(a_ref[...], w_ref[...]), gp_ref[...])


def proj_post(a, x, w, g_post, *, tm):
    t, kdim = a.shape
    d = x.shape[1]
    row = lambda i: (i, 0)
    fixed = lambda i: (0, 0)
    return pl.pallas_call(
        _proj_post_kernel,
        grid=(t // tm,),
        in_specs=[pl.BlockSpec((tm, kdim), row), pl.BlockSpec((tm, d), row), pl.BlockSpec((kdim, d), fixed),
                  pl.BlockSpec((1, d), fixed)],
        out_specs=pl.BlockSpec((tm, d), row),
        out_shape=jax.ShapeDtypeStruct((t, d), F32),
        compiler_params=_params("parallel"),
        name="proj_post",
    )(a, x, w, g_post)


def _sample_q_kernel(qa_ref, cos_ref, sin_ref, wq_ref, wuk_ref, o_ref, *, nope, rope, scale):
    q = _dot(qa_ref[...], wq_ref[...])
    qn = q[:, :nope].astype(BF16)
    qr = (q[:, nope:nope + rope] * cos_ref[...] + q[:, nope + rope:] * sin_ref[...]) * scale
    qlat = _dot_nt(qn, wuk_ref[...]) * scale
    o_ref[0] = jnp.concatenate([qlat, qr], axis=1).astype(BF16)


def sample_q(qa, cos, sin, wq, wuk, *, heads, scale):
    t, q_lora = qa.shape
    rope = cos.shape[1]
    kv_lora = wuk.shape[0]
    nope = wuk.shape[1] // heads
    fixed = lambda h: (0, 0)
    head = lambda h: (0, h)
    kern = functools.partial(_sample_q_kernel, nope=nope, rope=rope, scale=scale)
    return pl.pallas_call(
        kern,
        grid=(heads,),
        in_specs=[pl.BlockSpec((t, q_lora), fixed), pl.BlockSpec((t, rope), fixed),
                  pl.BlockSpec((t, rope), fixed),
                  pl.BlockSpec((q_lora, nope + 2 * rope), head), pl.BlockSpec((kv_lora, nope), head)],
        out_specs=pl.BlockSpec((1, t, kv_lora + rope), lambda h: (h, 0, 0)),
        out_shape=jax.ShapeDtypeStruct((heads, t, kv_lora + rope), BF16),
        compiler_params=_params("parallel"),
        name="sample_q",
    )(qa, cos, sin, wq, wuk)


DECODE_SLOTS = 3


def _decode_kernel(pt_ref, q_ref, cn_ref, rn_ref, cc_hbm, cr_hbm, o_ref, cbuf, rbuf, sem, m_ref, l_ref, acc_ref,
                   *, layer, pages, heads):
    ci = pl.program_id(1)
    n_chunks = pl.num_programs(1)
    step = pl.program_id(0) * n_chunks + ci
    n_steps = pl.num_programs(0) * n_chunks
    page = cc_hbm.shape[2]
    kv_lora = cc_hbm.shape[3]

    def page_copies(s, j):
        slot = s % DECODE_SLOTS
        pg = pt_ref[s * pages + j]
        rows = pl.ds(j * page, page)
        return (pltpu.make_async_copy(cc_hbm.at[layer, pg], cbuf.at[slot, rows], sem.at[0, slot]),
                pltpu.make_async_copy(cr_hbm.at[layer, pg], rbuf.at[slot, rows], sem.at[1, slot]))

    def start_step(s):
        for j in range(pages):
            for cp in page_copies(s, j):
                cp.start()

    @pl.when(step == 0)
    def _():
        for s in range(DECODE_SLOTS - 1):
            start_step(s)

    ahead = step + DECODE_SLOTS - 1

    @pl.when(ahead < n_steps)
    def _():
        start_step(ahead)

    for j in range(pages):
        for cp in page_copies(step, j):
            cp.wait()

    @pl.when(ci == 0)
    def _():
        m_ref[...] = jnp.full(m_ref.shape, -jnp.inf, F32)
        l_ref[...] = jnp.zeros(l_ref.shape, F32)
        acc_ref[...] = jnp.zeros(acc_ref.shape, F32)

    slot = step % DECODE_SLOTS
    q = q_ref[0]
    qlat, qr = q[:, :kv_lora], q[:, kv_lora:]
    kc = cbuf[slot].astype(BF16)
    s = _dot_nt(qlat, kc) + _dot_nt(qr, rbuf[slot].astype(BF16))
    m = m_ref[...]
    m_new = jnp.maximum(m, jnp.max(s, axis=-1, keepdims=True))
    alpha = jnp.exp(m - m_new)
    p = jnp.exp(s - m_new)
    l_ref[...] = alpha * l_ref[...] + jnp.sum(p, axis=-1, keepdims=True)
    acc_ref[...] = alpha * acc_ref[...] + _dot(p.astype(BF16), kc)
    m_ref[...] = m_new

    @pl.when(ci == pl.num_programs(1) - 1)
    def _():
        n_new = cn_ref.shape[1]
        cn = cn_ref[0].astype(BF16).astype(F32)
        rn = rn_ref[0].astype(BF16).astype(F32)
        qlf, qrf = qlat.astype(F32), qr.astype(F32)
        row = lax.broadcasted_iota(jnp.int32, (q.shape[0], 1), 0)
        cols = []
        for j in range(n_new):
            sj = (jnp.sum(qlf * cn[j:j + 1], axis=-1, keepdims=True)
                  + jnp.sum(qrf * rn[j:j + 1], axis=-1, keepdims=True))
            cols.append(jnp.where(row >= j * heads, sj, -jnp.inf))
        m0 = m_ref[...]
        m1 = m0
        for sj in cols:
            m1 = jnp.maximum(m1, sj)
        a = jnp.exp(m0 - m1)
        l1 = a * l_ref[...]
        acc1 = a * acc_ref[...]
        for j, sj in enumerate(cols):
            pj = jnp.exp(sj - m1)
            l1 = l1 + pj
            acc1 = acc1 + pj.astype(BF16).astype(F32) * cn[j:j + 1]
        o_ref[0] = acc1 / l1


def decode_attention(page_table, qs, c_new, r_new, cache_c, cache_r, *, layer, pages, heads):
    n, rows, dq = qs.shape
    n_pages = page_table.shape[1]
    _, _, page, kv_lora = cache_c.shape
    rope = cache_r.shape[3]
    n_new = c_new.shape[1]
    n_chunks = n_pages // pages
    assert n_pages % pages == 0 and n * n_chunks >= DECODE_SLOTS - 1
    pt = page_table.reshape(-1)
    seq = lambda i, c, pt_ref: (i, 0, 0)
    grid_spec = pltpu.PrefetchScalarGridSpec(
        num_scalar_prefetch=1,
        grid=(n, n_chunks),
        in_specs=[pl.BlockSpec((1, rows, dq), seq), pl.BlockSpec((1, n_new, kv_lora), seq),
                  pl.BlockSpec((1, n_new, rope), seq),
                  pl.BlockSpec(memory_space=pl.ANY), pl.BlockSpec(memory_space=pl.ANY)],
        out_specs=pl.BlockSpec((1, rows, kv_lora), seq),
        scratch_shapes=[pltpu.VMEM((DECODE_SLOTS, pages * page, kv_lora), F32),
                        pltpu.VMEM((DECODE_SLOTS, pages * page, rope), F32),
                        pltpu.SemaphoreType.DMA((2, DECODE_SLOTS)),
                        pltpu.VMEM((rows, 1), F32), pltpu.VMEM((rows, 1), F32),
                        pltpu.VMEM((rows, kv_lora), F32)],
    )
    return pl.pallas_call(
        functools.partial(_decode_kernel, layer=layer, pages=pages, heads=heads),
        grid_spec=grid_spec,
        out_shape=jax.ShapeDtypeStruct((n, rows, kv_lora), F32),
        compiler_params=_params("arbitrary", "arbitrary"),
        name="decode_attention",
    )(pt, qs, c_new, r_new, cache_c, cache_r)


def _sample_uv_kernel(ol_ref, wuv_ref, o_ref):
    o_ref[...] = _dot(ol_ref[0].astype(BF16), wuv_ref[...]).astype(o_ref.dtype)


def sample_uv(o_lat, wuv):
    heads, t, kv_lora = o_lat.shape
    vdim = wuv.shape[1] // heads
    return pl.pallas_call(
        _sample_uv_kernel,
        grid=(heads,),
        in_specs=[pl.BlockSpec((1, t, kv_lora), lambda h: (h, 0, 0)), pl.BlockSpec((kv_lora, vdim), lambda h: (0, h))],
        out_specs=pl.BlockSpec((t, vdim), lambda h: (0, h)),
        out_shape=jax.ShapeDtypeStruct((t, heads * vdim), BF16),
        compiler_params=_params("parallel"),
        name="sample_uv",
    )(o_lat, wuv)


def _rot_cols(w, half):
    return jnp.concatenate([-w[..., half:], w[..., :half]], axis=-1)


def _rope_tables(pos, rope):
    inv_freq = ROPE_BASE ** (-jnp.arange(0, rope, 2, dtype=F32) / rope)
    ang = pos.astype(F32)[:, None] * inv_freq[None, :]
    cos, sin = jnp.cos(ang), jnp.sin(ang)
    return jnp.concatenate([cos, cos], axis=-1), jnp.concatenate([sin, sin], axis=-1)


def kernel(x_prompt, x_sample, cache_kv_latent, cache_k_rope, state_conv, state_ffn_conv, page_table, norm_mix_pre, norm_mix_post, norm_ffn_pre, norm_ffn_post, cv_w_in, cv_b_in, cv_dw, cv_dw_b, cv_ln_g, cv_ln_b, cv_w_out, cv_b_out, mla_w_in, mla_q_norm, mla_kv_norm, mla_w_qb, mla_w_uk, mla_w_uv, mla_w_o, ffn_w_gate, ffn_w_up, ffn_dw, ffn_dw_b, ffn_w_down):
    batch, seq, d = x_prompt.shape
    n_dec, l_dec, _ = x_sample.shape
    depth = norm_mix_pre.shape[0]
    conv_w = cv_dw.shape[1]
    ffn_w = ffn_dw.shape[1]
    heads, nope = mla_w_uk.shape[2], mla_w_uk.shape[3]
    q_lora, kv_lora = mla_q_norm.shape[1], mla_kv_norm.shape[1]
    rope = mla_w_in.shape[2] - q_lora - kv_lora
    vdim = mla_w_uv.shape[3]
    dff = ffn_w_gate.shape[2]
    past = page_table.shape[1] * cache_kv_latent.shape[2]
    scale = 1.0 / math.sqrt(nope + rope)
    tp = batch * seq
    ts = n_dec * l_dec

    tm = min(512, seq)
    tm_ffn = min(1024, seq)
    tf = 512 if dff % 512 == 0 else dff
    tn = min(512, d)
    tm_conv = min(256, seq)
    pages = min(16, page_table.shape[1])
    flash_heads = 2

    vec = lambda a: a[:, None, :]
    n_pre, n_post, f_pre, f_post = vec(norm_mix_pre), vec(norm_mix_post), vec(norm_ffn_pre), vec(norm_ffn_post)
    bf = lambda a: a.astype(BF16)

    cos_p, sin_p = _rope_tables(jnp.tile(jnp.arange(seq), batch), rope)
    cos_s, sin_s = _rope_tables(jnp.repeat(past + jnp.arange(l_dec), n_dec), rope)

    xp = x_prompt.reshape(tp, d)
    kv_p, kr_p, conv_p, ffn_p = [], [], [], []
    xs = x_sample.transpose(1, 0, 2).reshape(ts, d)
    kv_s, kr_s, conv_s, ffn_s = [], [], [], []

    for i in range(depth):
        j = i // 2
        if i % 2 == 0:
            w_in_b = bf(cv_w_in[j])
            b_in = cv_b_in[j][None, :]
            args = (cv_dw[j], cv_dw_b[j][None, :], cv_ln_g[j][None, :], cv_ln_b[j][None, :], bf(cv_w_out[j]),
                    cv_b_out[j][None, :], n_post[i])
            u = conv_in(xp, n_pre[i], w_in_b, b_in, tm=tm, tn=tn)
            conv_p.append(u.reshape(batch, seq, d)[:, seq - (conv_w - 1):])
            xp = conv_out_prompt(u, xp, *args, seq=seq, tm=tm_conv)

            us = conv_in(xs, n_pre[i], w_in_b, b_in, tm=ts, tn=tn)
            u_ext = jnp.concatenate([state_conv[j].transpose(1, 0, 2), us.reshape(l_dec, n_dec, d)], axis=0)
            conv_s.append(u_ext[l_dec:].transpose(1, 0, 2))
            xs = conv_out_sample(u_ext, xs.reshape(l_dec, n_dec, d), *args, ns=32).reshape(ts, d)
        else:
            w_in = mla_w_in[j]
            w_in_ext = bf(jnp.concatenate([w_in, _rot_cols(w_in[:, q_lora + kv_lora:], rope // 2)], axis=-1))
            w_qb = mla_w_qb[j]
            wq = bf(jnp.concatenate([w_qb, _rot_cols(w_qb[..., nope:], rope // 2)], axis=-1))
            wq = wq.reshape(q_lora, heads * (nope + 2 * rope))
            wuk = bf(mla_w_uk[j]).reshape(kv_lora, heads * nope)
            wuv = bf(mla_w_uv[j]).reshape(kv_lora, heads * vdim)
            w_o = bf(mla_w_o[j]).reshape(heads * vdim, d)
            w = (w_in_ext, mla_q_norm[j][None, :], mla_kv_norm[j][None, :])

            qa, ckv, kr = mla_in(xp, n_pre[i], *w, cos_p, sin_p, tm=tm)
            kv_p.append(ckv.reshape(batch, seq, kv_lora))
            kr_p.append(kr.reshape(batch, seq, rope))
            q, k, v = mla_qkv(qa, ckv, kr, cos_p, sin_p, wq, wuk, wuv, heads=heads, tm=tm, scale=scale)
            o = flash_prompt(q.reshape(batch, seq, -1), k.reshape(batch, seq, -1), v.reshape(batch, seq, -1),
                             heads=heads, tq=tm, hb=flash_heads)
            xp = proj_post(o.reshape(tp, heads * vdim), xp, w_o, n_post[i], tm=tm)

            qa, ckv, kr = mla_in(xs, n_pre[i], *w, cos_s, sin_s, tm=ts)
            c_new = ckv.reshape(l_dec, n_dec, kv_lora).transpose(1, 0, 2)
            r_new = kr.reshape(l_dec, n_dec, rope).transpose(1, 0, 2)
            kv_s.append(c_new)
            kr_s.append(r_new)
            qs = sample_q(qa, cos_s, sin_s, wq, wuk, heads=heads, scale=scale)
            qs = qs.reshape(heads, l_dec, n_dec, kv_lora + rope).transpose(2, 1, 0, 3)
            qs = qs.reshape(n_dec, l_dec * heads, kv_lora + rope)
            o_lat = decode_attention(page_table, qs, c_new, r_new, cache_kv_latent, cache_k_rope,
                                     layer=j, pages=pages, heads=heads)
            o_lat = o_lat.reshape(n_dec, l_dec, heads, kv_lora).transpose(2, 1, 0, 3).reshape(heads, ts, kv_lora)
            xs = proj_post(sample_uv(o_lat, wuv), xs, w_o, n_post[i], tm=ts)

        ffn_w_args = (bf(ffn_w_gate[i]), bf(ffn_w_up[i]), ffn_dw[i], ffn_dw_b[i][None, :], bf(ffn_w_down[i]))
        xp, g_tail = ffn_prompt(xp, f_pre[i], f_post[i], *ffn_w_args, seq=seq, tm=tm_ffn, tf=tf)
        g_tail = g_tail.reshape(batch, seq // tm_ffn, FFN_TAIL, dff)
        ffn_p.append(g_tail[:, -1, FFN_TAIL - (ffn_w - 1):])
        xs, g_new = ffn_sample(xs, f_pre[i], f_post[i], state_ffn_conv[i].transpose(1, 0, 2), *ffn_w_args, tf=tf)
        ffn_s.append(g_new.transpose(1, 0, 2))

    y_prompt = xp.reshape(batch, seq, d)
    y_sample = xs.reshape(l_dec, n_dec, d).transpose(1, 0, 2)
    return (y_prompt, y_sample, jnp.stack(kv_p), jnp.stack(kr_p), jnp.stack(kv_s), jnp.stack(kr_s),
            jnp.stack(conv_p), jnp.stack(conv_s), jnp.stack(ffn_p), jnp.stack(ffn_s))
```

```python
import functools
import math

import jax
import jax.numpy as jnp
from jax import lax
from jax.experimental import pallas as pl
from jax.experimental.pallas import tpu as pltpu

EPS = 1e-6
ROPE_BASE = 10000.0
F32 = jnp.float32
BF16 = jnp.bfloat16

V7X_VMEM_LIMIT_BYTES = 56 * 1024 * 1024


def _params(*sem):
    return pltpu.CompilerParams(dimension_semantics=sem, vmem_limit_bytes=V7X_VMEM_LIMIT_BYTES)


def _dot(a, b):
    return jnp.dot(a, b, preferred_element_type=F32)


def _dot_nt(a, b):
    return lax.dot_general(a, b, (((1,), (1,)), ((), ())), preferred_element_type=F32)


def _rms(xf, g):
    ms = jnp.mean(xf * xf, axis=-1, keepdims=True)
    return (xf * lax.rsqrt(ms + EPS)) * g


def _silu(x):
    return x * jax.nn.sigmoid(x)


def _conv_in_kernel(x_ref, g_ref, w1_ref, w2_ref, b1_ref, b2_ref, u_ref, h_ref):
    @pl.when(pl.program_id(1) == 0)
    def _():
        h_ref[...] = _rms(x_ref[...], g_ref[...]).astype(BF16)

    h = h_ref[...]
    a1 = _dot(h, w1_ref[...]) + b1_ref[...]
    a2 = _dot(h, w2_ref[...]) + b2_ref[...]
    u_ref[...] = a1 * jax.nn.sigmoid(a2)


def conv_in(x, g_pre, w_in, b_in, *, layer, tm, tn):
    t, d = x.shape
    nj = d // tn
    return pl.pallas_call(
        _conv_in_kernel,
        grid=(t // tm, nj),
        in_specs=[
            pl.BlockSpec((tm, d), lambda i, j: (i, 0)),
            pl.BlockSpec((1, d), lambda i, j: (0, 0)),
            pl.BlockSpec((None, d, tn), lambda i, j: (layer, 0, j)),
            pl.BlockSpec((None, d, tn), lambda i, j: (layer, 0, j + nj)),
            pl.BlockSpec((1, tn), lambda i, j: (0, j)),
            pl.BlockSpec((1, tn), lambda i, j: (0, j + nj)),
        ],
        out_specs=pl.BlockSpec((tm, tn), lambda i, j: (i, j)),
        out_shape=jax.ShapeDtypeStruct((t, d), F32),
        scratch_shapes=[pltpu.VMEM((tm, d), BF16)],
        compiler_params=_params("parallel", "arbitrary"),
        name="conv_in",
    )(x, g_pre, w_in, w_in, b_in, b_in)


SUBLANES = 8
CONV_HALO = 32


def _conv_out_prompt_kernel(u_ref, uh_ref, x_ref, dw_ref, dwb_ref, lng_ref, lnb_ref, w_ref, b_ref, gp_ref,
                            o_ref, ext_ref, c_ref, sh_ref, *, tiles_per_seq, width, lane_chunk):
    tm, d = u_ref.shape
    first = pl.program_id(0) % tiles_per_seq == 0
    ext_ref[0:CONV_HALO, :] = jnp.where(first, 0.0, uh_ref[...])
    ext_ref[CONV_HALO:, :] = u_ref[...]
    base = CONV_HALO - (width - 1)
    n_sh = sh_ref.shape[1]

    def chunk(c, carry):
        col = pl.ds(pl.multiple_of(c * lane_chunk, lane_chunk), lane_chunk)
        for b in range(1, SUBLANES):
            sh_ref[b - 1] = ext_ref[pl.ds(b, n_sh), col]
        acc = jnp.zeros((tm, lane_chunk), F32)
        for k in range(width):
            b = (base + k) % SUBLANES
            a8 = base + k - b
            tap = ext_ref[pl.ds(a8, tm), col] if b == 0 else sh_ref[b - 1, pl.ds(a8, tm), :]
            acc = acc + tap * dw_ref[pl.ds(k, 1), col]
        c_ref[:, col] = acc + dwb_ref[:, col]
        return carry

    lax.fori_loop(0, d // lane_chunk, chunk, 0)
    c = c_ref[...]
    mu = jnp.mean(c, axis=-1, keepdims=True)
    xc = c - mu
    var = jnp.mean(xc * xc, axis=-1, keepdims=True)
    y = xc * lax.rsqrt(var + EPS) * lng_ref[...] + lnb_ref[...]
    z = _dot(_silu(y).astype(BF16), w_ref[...]) + b_ref[...]
    o_ref[...] = x_ref[...] + _rms(z, gp_ref[...])


def conv_out_prompt(u, x, dw, dw_b, ln_g, ln_b, w_out, b_out, g_post, *, layer, seq, tm):
    t, d = u.shape
    width = dw.shape[0]
    hb = tm // CONV_HALO
    row = lambda i: (i, 0)
    fixed = lambda i: (0, 0)
    kern = functools.partial(_conv_out_prompt_kernel, tiles_per_seq=seq // tm, width=width, lane_chunk=128)
    return pl.pallas_call(
        kern,
        grid=(t // tm,),
        in_specs=[
            pl.BlockSpec((tm, d), row),
            pl.BlockSpec((CONV_HALO, d), lambda i: (jnp.maximum(i * hb - 1, 0), 0)),
            pl.BlockSpec((tm, d), row),
            pl.BlockSpec((width, d), fixed),
            pl.BlockSpec((1, d), fixed),
            pl.BlockSpec((1, d), fixed),
            pl.BlockSpec((1, d), fixed),
            pl.BlockSpec((None, d, d), lambda i: (layer, 0, 0)),
            pl.BlockSpec((1, d), fixed),
            pl.BlockSpec((1, d), fixed),
        ],
        out_specs=pl.BlockSpec((tm, d), row),
        out_shape=jax.ShapeDtypeStruct((t, d), F32),
        scratch_shapes=[pltpu.VMEM((tm + CONV_HALO, d), F32), pltpu.VMEM((tm, d), F32),
                        pltpu.VMEM((SUBLANES - 1, tm + CONV_HALO - SUBLANES, 128), F32)],
        compiler_params=_params("parallel"),
        name="conv_out_prompt",
    )(u, u, x, dw, dw_b, ln_g, ln_b, w_out, b_out, g_post)


def _conv_out_sample_kernel(ue_ref, x_ref, dw_ref, dwb_ref, lng_ref, lnb_ref, w_ref, b_ref, gp_ref, o_ref,
                            *, width):
    n_tok, ns, d = x_ref.shape
    acts = []
    for t in range(n_tok):
        c = jnp.zeros((ns, d), F32)
        for k in range(width):
            c = c + ue_ref[t + k] * dw_ref[pl.ds(k, 1), :]
        c = c + dwb_ref[...]
        mu = jnp.mean(c, axis=-1, keepdims=True)
        xc = c - mu
        var = jnp.mean(xc * xc, axis=-1, keepdims=True)
        y = xc * lax.rsqrt(var + EPS) * lng_ref[...] + lnb_ref[...]
        acts.append(_silu(y).astype(BF16))
    z = _dot(jnp.concatenate(acts, axis=0), w_ref[...]) + b_ref[...]
    r = _rms(z, gp_ref[...])
    for t in range(n_tok):
        o_ref[t] = x_ref[t] + r[t * ns:(t + 1) * ns]


def conv_out_sample(u_ext, x, dw, dw_b, ln_g, ln_b, w_out, b_out, g_post, *, layer, ns):
    rows, n, d = u_ext.shape
    n_tok = x.shape[0]
    width = dw.shape[0]
    fixed = lambda i: (0, 0)
    return pl.pallas_call(
        functools.partial(_conv_out_sample_kernel, width=width),
        grid=(n // ns,),
        in_specs=[
            pl.BlockSpec((rows, ns, d), lambda i: (0, i, 0)),
            pl.BlockSpec((n_tok, ns, d), lambda i: (0, i, 0)),
            pl.BlockSpec((width, d), fixed),
            pl.BlockSpec((1, d), fixed),
            pl.BlockSpec((1, d), fixed),
            pl.BlockSpec((1, d), fixed),
            pl.BlockSpec((None, d, d), lambda i: (layer, 0, 0)),
            pl.BlockSpec((1, d), fixed),
            pl.BlockSpec((1, d), fixed),
        ],
        out_specs=pl.BlockSpec((n_tok, ns, d), lambda i: (0, i, 0)),
        out_shape=jax.ShapeDtypeStruct(x.shape, F32),
        compiler_params=_params("parallel"),
        name="conv_out_sample",
    )(u_ext, x, dw, dw_b, ln_g, ln_b, w_out, b_out, g_post)


FFN_TAIL = 8


def _ffn_finish(x_ref, gpost_ref, acc_ref, o_ref):
    o_ref[...] = x_ref[...] + _rms(acc_ref[...], gpost_ref[...])


def _ffn_prompt_kernel(x_ref, gpre_ref, gpost_ref, wg_ref, wu_ref, dw_ref, dwb_ref, wd_ref,
                       o_ref, gt_ref, h_ref, carry_ref, *, tiles_per_seq):
    acc_ref = o_ref
    i, f = pl.program_id(0), pl.program_id(1)
    tm = x_ref.shape[0]

    @pl.when(f == 0)
    def _():
        h_ref[...] = _rms(x_ref[...], gpre_ref[...]).astype(BF16)
        acc_ref[...] = jnp.zeros_like(acc_ref)

    @pl.when(i % tiles_per_seq == 0)
    def _():
        carry_ref[f] = jnp.zeros(carry_ref.shape[1:], F32)

    h = h_ref[...]
    g = _dot(h, wg_ref[...])
    u = _dot(h, wu_ref[...])
    prev = carry_ref[f]
    p1 = prev[FFN_TAIL - 1:FFN_TAIL]
    p2 = prev[FFN_TAIL - 2:FFN_TAIL - 1]
    row = lax.broadcasted_iota(jnp.int32, g.shape, 0)
    g1 = jnp.where(row == 0, p1, pltpu.roll(g, 1, 0))
    g2 = jnp.where(row == 0, p2, jnp.where(row == 1, p1, pltpu.roll(g, 2, 0)))
    gc = g2 * dw_ref[0:1, :] + g1 * dw_ref[1:2, :] + g * dw_ref[2:3, :] + dwb_ref[...]
    act = (_silu(gc) * u).astype(BF16)
    acc_ref[...] += _dot(act, wd_ref[...])
    tail = g[tm - FFN_TAIL:, :]
    carry_ref[f] = tail
    gt_ref[...] = tail

    @pl.when(f == pl.num_programs(1) - 1)
    def _():
        _ffn_finish(x_ref, gpost_ref, acc_ref, o_ref)


def ffn_prompt(x, g_pre, g_post, w_gate, w_up, dw, dw_b, w_down, *, layer, seq, tm, tf):
    t, d = x.shape
    dff = w_gate.shape[2]
    nm, nf = t // tm, dff // tf
    kern = functools.partial(_ffn_prompt_kernel, tiles_per_seq=seq // tm)
    return pl.pallas_call(
        kern,
        grid=(nm, nf),
        in_specs=[
            pl.BlockSpec((tm, d), lambda i, f: (i, 0), pipeline_mode=pl.Buffered(1)),
            pl.BlockSpec((1, d), lambda i, f: (0, 0)),
            pl.BlockSpec((1, d), lambda i, f: (0, 0)),
            pl.BlockSpec((None, d, tf), lambda i, f: (layer, 0, f)),
            pl.BlockSpec((None, d, tf), lambda i, f: (layer, 0, f)),
            pl.BlockSpec((dw.shape[0], tf), lambda i, f: (0, f)),
            pl.BlockSpec((1, tf), lambda i, f: (0, f)),
            pl.BlockSpec((None, tf, d), lambda i, f: (layer, f, 0)),
        ],
        out_specs=[
            pl.BlockSpec((tm, d), lambda i, f: (i, 0)),
            pl.BlockSpec((FFN_TAIL, tf), lambda i, f: (i, f)),
        ],
        out_shape=[jax.ShapeDtypeStruct((t, d), F32), jax.ShapeDtypeStruct((nm * FFN_TAIL, dff), F32)],
        scratch_shapes=[pltpu.VMEM((tm, d), BF16), pltpu.VMEM((nf, FFN_TAIL, tf), F32)],
        compiler_params=_params("arbitrary", "arbitrary"),
        name="ffn_prompt",
    )(x, g_pre, g_post, w_gate, w_up, dw, dw_b, w_down)


def _ffn_sample_kernel(x_ref, gpre_ref, gpost_ref, st_ref, wg_ref, wu_ref, dw_ref, dwb_ref, wd_ref,
                       o_ref, gn_ref, h_ref, acc_ref):
    f = pl.program_id(0)
    n_state, n = st_ref.shape[0], st_ref.shape[1]
    t_rows = x_ref.shape[0]

    @pl.when(f == 0)
    def _():
        h_ref[...] = _rms(x_ref[...], gpre_ref[...]).astype(BF16)
        acc_ref[...] = jnp.zeros_like(acc_ref)

    h = h_ref[...]
    g = _dot(h, wg_ref[...])
    u = _dot(h, wu_ref[...])
    s0, s1 = st_ref[0], st_ref[1]
    g1 = jnp.concatenate([s1, g[:t_rows - n]], axis=0)
    g2 = jnp.concatenate([s0, s1, g[:t_rows - 2 * n]], axis=0)
    gc = g2 * dw_ref[0:1, :] + g1 * dw_ref[1:2, :] + g * dw_ref[2:3, :] + dwb_ref[...]
    act = (_silu(gc) * u).astype(BF16)
    acc_ref[...] += _dot(act, wd_ref[...])
    for s in range(n_state):
        gn_ref[s] = g[t_rows - (n_state - s) * n:t_rows - (n_state - s - 1) * n]

    @pl.when(f == pl.num_programs(0) - 1)
    def _():
        _ffn_finish(x_ref, gpost_ref, acc_ref, o_ref)


def ffn_sample(x, g_pre, g_post, state, w_gate, w_up, dw, dw_b, w_down, *, layer, tf):
    t, d = x.shape
    dff = w_gate.shape[2]
    n_state, n, _ = state.shape
    assert dw.shape[0] == 3 and n_state == 2 and t >= 2 * n
    return pl.pallas_call(
        _ffn_sample_kernel,
        grid=(dff // tf,),
        in_specs=[
            pl.BlockSpec((t, d), lambda f: (0, 0)),
            pl.BlockSpec((1, d), lambda f: (0, 0)),
            pl.BlockSpec((1, d), lambda f: (0, 0)),
            pl.BlockSpec((n_state, n, tf), lambda f: (0, 0, f)),
            pl.BlockSpec((None, d, tf), lambda f: (layer, 0, f)),
            pl.BlockSpec((None, d, tf), lambda f: (layer, 0, f)),
            pl.BlockSpec((3, tf), lambda f: (0, f)),
            pl.BlockSpec((1, tf), lambda f: (0, f)),
            pl.BlockSpec((None, tf, d), lambda f: (layer, f, 0)),
        ],
        out_specs=[
            pl.BlockSpec((t, d), lambda f: (0, 0)),
            pl.BlockSpec((n_state, n, tf), lambda f: (0, 0, f)),
        ],
        out_shape=[jax.ShapeDtypeStruct((t, d), F32), jax.ShapeDtypeStruct(state.shape, F32)],
        scratch_shapes=[pltpu.VMEM((t, d), BF16), pltpu.VMEM((t, d), F32)],
        compiler_params=_params("arbitrary"),
        name="ffn_sample",
    )(x, g_pre, g_post, state, w_gate, w_up, dw, dw_b, w_down)


def _mla_in_kernel(x_ref, g_ref, w_ref, qn_ref, kvn_ref, cos_ref, sin_ref, qa_ref, ckv_ref, kr_ref,
                   *, q_lora, kv_lora, rope):
    h = _rms(x_ref[...], g_ref[...]).astype(BF16)
    z = _dot(h, w_ref[...])
    qa_ref[...] = _rms(z[:, :q_lora], qn_ref[...]).astype(BF16)
    ckv_ref[...] = _rms(z[:, q_lora:q_lora + kv_lora], kvn_ref[...])
    o = q_lora + kv_lora
    kr_ref[...] = z[:, o:o + rope] * cos_ref[...] + z[:, o + rope:o + 2 * rope] * sin_ref[...]


def mla_in(x, g_pre, w_ext, q_norm, kv_norm, cos, sin, *, tm):
    t, d = x.shape
    q_lora, kv_lora, rope = q_norm.shape[1], kv_norm.shape[1], cos.shape[1]
    n_ext = w_ext.shape[1]
    row = lambda i: (i, 0)
    fixed = lambda i: (0, 0)
    kern = functools.partial(_mla_in_kernel, q_lora=q_lora, kv_lora=kv_lora, rope=rope)
    return pl.pallas_call(
        kern,
        grid=(t // tm,),
        in_specs=[
            pl.BlockSpec((tm, d), row),
            pl.BlockSpec((1, d), fixed),
            pl.BlockSpec((d, n_ext), fixed),
            pl.BlockSpec((1, q_lora), fixed),
            pl.BlockSpec((1, kv_lora), fixed),
            pl.BlockSpec((tm, rope), row),
            pl.BlockSpec((tm, rope), row),
        ],
        out_specs=[pl.BlockSpec((tm, q_lora), row), pl.BlockSpec((tm, kv_lora), row),
                   pl.BlockSpec((tm, rope), row)],
        out_shape=[jax.ShapeDtypeStruct((t, q_lora), BF16), jax.ShapeDtypeStruct((t, kv_lora), F32),
                   jax.ShapeDtypeStruct((t, rope), F32)],
        compiler_params=_params("parallel"),
        name="mla_in",
    )(x, g_pre, w_ext, q_norm, kv_norm, cos, sin)


def _mla_qkv_kernel(qa_ref, ckv_ref, kr_ref, cos_ref, sin_ref, wq_ref, wuk_ref, wuv_ref,
                    q_ref, k_ref, v_ref, *, heads, nope, rope, scale):
    tm = qa_ref.shape[0]
    hd = nope + 2 * rope
    qa = qa_ref[...]
    ckv = ckv_ref[...].astype(BF16)
    cos, sin = cos_ref[...], sin_ref[...]
    kr = kr_ref[...].astype(BF16)
    pad = jnp.zeros((tm, hd - nope - rope), BF16)
    for h in range(heads):
        q = _dot(qa, wq_ref[:, h * hd:(h + 1) * hd])
        qn = q[:, :nope] * scale
        qr = (q[:, nope:nope + rope] * cos + q[:, nope + rope:] * sin) * scale
        q_ref[:, h * hd:h * hd + nope] = qn.astype(BF16)
        q_ref[:, h * hd + nope:h * hd + nope + rope] = qr.astype(BF16)
        q_ref[:, h * hd + nope + rope:(h + 1) * hd] = pad
    for hp in range(heads // 2):
        kn2 = _dot(ckv, wuk_ref[:, 2 * hp * nope:2 * (hp + 1) * nope])
        for j in range(2):
            h = 2 * hp + j
            k_ref[:, h * hd:h * hd + nope] = kn2[:, j * nope:(j + 1) * nope].astype(BF16)
            k_ref[:, h * hd + nope:h * hd + nope + rope] = kr
            k_ref[:, h * hd + nope + rope:(h + 1) * hd] = pad
    v_ref[...] = _dot(ckv, wuv_ref[...]).astype(BF16)


def mla_qkv(qa, ckv, kr, cos, sin, wq, wuk, wuv, *, heads, tm, scale):
    t, q_lora = qa.shape
    kv_lora, rope = ckv.shape[1], kr.shape[1]
    nope = wuk.shape[1] // heads
    hd = nope + 2 * rope
    assert heads % 2 == 0 and wq.shape[1] == heads * hd
    row = lambda i: (i, 0)
    fixed = lambda i: (0, 0)
    kern = functools.partial(_mla_qkv_kernel, heads=heads, nope=nope, rope=rope, scale=scale)
    return pl.pallas_call(
        kern,
        grid=(t // tm,),
        in_specs=[
            pl.BlockSpec((tm, q_lora), row),
            pl.BlockSpec((tm, kv_lora), row),
            pl.BlockSpec((tm, rope), row),
            pl.BlockSpec((tm, rope), row),
            pl.BlockSpec((tm, rope), row),
            pl.BlockSpec(wq.shape, fixed),
            pl.BlockSpec(wuk.shape, fixed),
            pl.BlockSpec(wuv.shape, fixed),
        ],
        out_specs=[pl.BlockSpec((tm, heads * hd), row), pl.BlockSpec((tm, heads * hd), row),
                   pl.BlockSpec((tm, wuv.shape[1]), row)],
        out_shape=[jax.ShapeDtypeStruct((t, heads * hd), BF16), jax.ShapeDtypeStruct((t, heads * hd), BF16),
                   jax.ShapeDtypeStruct((t, wuv.shape[1]), BF16)],
        compiler_params=_params("parallel"),
        name="mla_qkv",
    )(qa, ckv, kr, cos, sin, wq, wuk, wuv)


def _flash_kernel(q_ref, k_ref, v_ref, o_ref, *, tk, hb):
    tq = q_ref.shape[1]
    hd = q_ref.shape[2] // hb
    vdim = v_ref.shape[2] // hb
    qi = pl.program_id(2)
    qs = [q_ref[0, :, h * hd:(h + 1) * hd] for h in range(hb)]

    def step(h, kb, carry, masked):
        m, l, acc = carry
        rows = pl.ds(pl.multiple_of(kb * tk, tk), tk)
        s = _dot_nt(qs[h], k_ref[0, rows, h * hd:(h + 1) * hd])
        if masked:
            r = lax.broadcasted_iota(jnp.int32, s.shape, 0)
            c = lax.broadcasted_iota(jnp.int32, s.shape, 1)
            s = jnp.where(c <= r, s, -jnp.inf)
        m_new = jnp.maximum(m, jnp.max(s, axis=-1, keepdims=True))
        alpha = jnp.exp(m - m_new)
        p = jnp.exp(s - m_new)
        l = alpha * l + jnp.sum(p, axis=-1, keepdims=True)
        acc = alpha * acc + _dot(p.astype(BF16), v_ref[0, rows, h * vdim:(h + 1) * vdim])
        return m_new, l, acc

    init1 = (jnp.full((tq, 1), -jnp.inf, F32), jnp.zeros((tq, 1), F32), jnp.zeros((tq, vdim), F32))
    body = lambda kb, cs: tuple(step(h, kb, cs[h], False) for h in range(hb))
    carries = lax.fori_loop(0, qi, body, tuple(init1 for _ in range(hb)))
    for h in range(hb):
        m, l, acc = step(h, qi, carries[h], True)
        o_ref[0, :, h * vdim:(h + 1) * vdim] = (acc / l).astype(o_ref.dtype)


def flash_prompt(q, k, v, *, heads, tq, hb):
    batch, seq, _ = q.shape
    hd = q.shape[2] // heads
    vdim = v.shape[2] // heads
    kern = functools.partial(_flash_kernel, tk=tq, hb=hb)
    return pl.pallas_call(
        kern,
        grid=(batch, heads // hb, seq // tq),
        in_specs=[
            pl.BlockSpec((1, tq, hb * hd), lambda n, g, i: (n, i, g)),
            pl.BlockSpec((1, seq, hb * hd), lambda n, g, i: (n, 0, g)),
            pl.BlockSpec((1, seq, hb * vdim), lambda n, g, i: (n, 0, g)),
        ],
        out_specs=pl.BlockSpec((1, tq, hb * vdim), lambda n, g, i: (n, i, g)),
        out_shape=jax.ShapeDtypeStruct((batch, seq, heads * vdim), BF16),
        compiler_params=_params("parallel", "parallel", "arbitrary"),
        name="flash_prompt",
    )(q, k, v)


def _proj_post_kernel(a_ref, x_ref, w_ref, gp_ref, o_ref):
    o_ref[...] = x_ref[...] + _rms(_dot(a_ref[...], w_ref[...]), gp_ref[...])


def proj_post(a, x, w, g_post, *, tm):
    t, kdim = a.shape
    d = x.shape[1]
    row = lambda i: (i, 0)
    fixed = lambda i: (0, 0)
    return pl.pallas_call(
        _proj_post_kernel,
        grid=(t // tm,),
        in_specs=[pl.BlockSpec((tm, kdim), row), pl.BlockSpec((tm, d), row), pl.BlockSpec((kdim, d), fixed),
                  pl.BlockSpec((1, d), fixed)],
        out_specs=pl.BlockSpec((tm, d), row),
        out_shape=jax.ShapeDtypeStruct((t, d), F32),
        compiler_params=_params("parallel"),
        name="proj_post",
    )(a, x, w, g_post)


def _sample_q_kernel(qa_ref, cos_ref, sin_ref, wq_ref, wuk_ref, o_ref, *, nope, rope, scale):
    q = _dot(qa_ref[...], wq_ref[...])
    qn = q[:, :nope].astype(BF16)
    qr = (q[:, nope:nope + rope] * cos_ref[...] + q[:, nope + rope:] * sin_ref[...]) * scale
    qlat = _dot_nt(qn, wuk_ref[...]) * scale
    o_ref[0] = jnp.concatenate([qlat, qr], axis=1).astype(BF16)


def sample_q(qa, cos, sin, wq, wuk, *, heads, scale):
    t, q_lora = qa.shape
    rope = cos.shape[1]
    kv_lora = wuk.shape[0]
    nope = wuk.shape[1] // heads
    fixed = lambda h: (0, 0)
    head = lambda h: (0, h)
    kern = functools.partial(_sample_q_kernel, nope=nope, rope=rope, scale=scale)
    return pl.pallas_call(
        kern,
        grid=(heads,),
        in_specs=[pl.BlockSpec((t, q_lora), fixed), pl.BlockSpec((t, rope), fixed),
                  pl.BlockSpec((t, rope), fixed),
                  pl.BlockSpec((q_lora, nope + 2 * rope), head), pl.BlockSpec((kv_lora, nope), head)],
        out_specs=pl.BlockSpec((1, t, kv_lora + rope), lambda h: (h, 0, 0)),
        out_shape=jax.ShapeDtypeStruct((heads, t, kv_lora + rope), BF16),
        compiler_params=_params("parallel"),
        name="sample_q",
    )(qa, cos, sin, wq, wuk)


DECODE_SLOTS = 3


def _decode_kernel(pt_ref, q_ref, cn_ref, rn_ref, cc_hbm, crt_hbm, o_ref, cbuf, rtbuf, sem, m_ref, l_ref, acc_ref,
                   *, layer, pages, heads):
    ci = pl.program_id(1)
    n_chunks = pl.num_programs(1)
    step = pl.program_id(0) * n_chunks + ci
    n_steps = pl.num_programs(0) * n_chunks
    page = cc_hbm.shape[2]
    kv_lora = cc_hbm.shape[3]

    def page_copies(s, j):
        slot = s % DECODE_SLOTS
        pg = pt_ref[s * pages + j]
        rows = pl.ds(j * page, page)
        return (pltpu.make_async_copy(cc_hbm.at[layer, pg], cbuf.at[slot, rows], sem.at[0, slot]),
                pltpu.make_async_copy(crt_hbm.at[layer, pg], rtbuf.at[slot, :, rows], sem.at[1, slot]))

    def start_step(s):
        for j in range(pages):
            for cp in page_copies(s, j):
                cp.start()

    @pl.when(step == 0)
    def _():
        for s in range(DECODE_SLOTS - 1):
            start_step(s)

    ahead = step + DECODE_SLOTS - 1

    @pl.when(ahead < n_steps)
    def _():
        start_step(ahead)

    for j in range(pages):
        for cp in page_copies(step, j):
            cp.wait()

    @pl.when(ci == 0)
    def _():
        m_ref[...] = jnp.full(m_ref.shape, -jnp.inf, F32)
        l_ref[...] = jnp.zeros(l_ref.shape, F32)
        acc_ref[...] = jnp.zeros(acc_ref.shape, F32)

    slot = step % DECODE_SLOTS
    q = q_ref[0]
    qlat, qr = q[:, :kv_lora], q[:, kv_lora:]
    kc = cbuf[slot].astype(BF16)
    s = _dot_nt(qlat, kc) + _dot(qr, rtbuf[slot].astype(BF16))
    m = m_ref[...]
    m_new = jnp.maximum(m, jnp.max(s, axis=-1, keepdims=True))
    alpha = jnp.exp(m - m_new)
    p = jnp.exp(s - m_new)
    l_ref[...] = alpha * l_ref[...] + jnp.sum(p, axis=-1, keepdims=True)
    acc_ref[...] = alpha * acc_ref[...] + _dot(p.astype(BF16), kc)
    m_ref[...] = m_new

    @pl.when(ci == pl.num_programs(1) - 1)
    def _():
        n_new = cn_ref.shape[1]
        cn = cn_ref[0].astype(BF16).astype(F32)
        rn = rn_ref[0].astype(BF16).astype(F32)
        qlf, qrf = qlat.astype(F32), qr.astype(F32)
        row = lax.broadcasted_iota(jnp.int32, (q.shape[0], 1), 0)
        cols = []
        for j in range(n_new):
            sj = (jnp.sum(qlf * cn[j:j + 1], axis=-1, keepdims=True)
                  + jnp.sum(qrf * rn[j:j + 1], axis=-1, keepdims=True))
            cols.append(jnp.where(row >= j * heads, sj, -jnp.inf))
        m0 = m_ref[...]
        m1 = m0
        for sj in cols:
            m1 = jnp.maximum(m1, sj)
        a = jnp.exp(m0 - m1)
        l1 = a * l_ref[...]
        acc1 = a * acc_ref[...]
        for j, sj in enumerate(cols):
            pj = jnp.exp(sj - m1)
            l1 = l1 + pj
            acc1 = acc1 + pj.astype(BF16).astype(F32) * cn[j:j + 1]
        o_ref[0] = acc1 / l1


def decode_attention(page_table, qs, c_new, r_new, cache_c, cache_rt, *, layer, pages, heads):
    n, rows, dq = qs.shape
    n_pages = page_table.shape[1]
    _, _, page, kv_lora = cache_c.shape
    rope = cache_rt.shape[2]
    n_new = c_new.shape[1]
    n_chunks = n_pages // pages
    assert n_pages % pages == 0 and n * n_chunks >= DECODE_SLOTS - 1
    pt = page_table.reshape(-1)
    seq = lambda i, c, pt_ref: (i, 0, 0)
    grid_spec = pltpu.PrefetchScalarGridSpec(
        num_scalar_prefetch=1,
        grid=(n, n_chunks),
        in_specs=[pl.BlockSpec((1, rows, dq), seq), pl.BlockSpec((1, n_new, kv_lora), seq),
                  pl.BlockSpec((1, n_new, rope), seq),
                  pl.BlockSpec(memory_space=pl.ANY), pl.BlockSpec(memory_space=pl.ANY)],
        out_specs=pl.BlockSpec((1, rows, kv_lora), seq),
        scratch_shapes=[pltpu.VMEM((DECODE_SLOTS, pages * page, kv_lora), F32),
                        pltpu.VMEM((DECODE_SLOTS, rope, pages * page), F32),
                        pltpu.SemaphoreType.DMA((2, DECODE_SLOTS)),
                        pltpu.VMEM((rows, 1), F32), pltpu.VMEM((rows, 1), F32),
                        pltpu.VMEM((rows, kv_lora), F32)],
    )
    return pl.pallas_call(
        functools.partial(_decode_kernel, layer=layer, pages=pages, heads=heads),
        grid_spec=grid_spec,
        out_shape=jax.ShapeDtypeStruct((n, rows, kv_lora), F32),
        compiler_params=_params("arbitrary", "arbitrary"),
        name="decode_attention",
    )(pt, qs, c_new, r_new, cache_c, cache_rt)


def _sample_uv_kernel(ol_ref, wuv_ref, o_ref):
    o_ref[...] = _dot(ol_ref[0].astype(BF16), wuv_ref[...]).astype(o_ref.dtype)


def sample_uv(o_lat, wuv):
    heads, t, kv_lora = o_lat.shape
    vdim = wuv.shape[1] // heads
    return pl.pallas_call(
        _sample_uv_kernel,
        grid=(heads,),
        in_specs=[pl.BlockSpec((1, t, kv_lora), lambda h: (h, 0, 0)), pl.BlockSpec((kv_lora, vdim), lambda h: (0, h))],
        out_specs=pl.BlockSpec((t, vdim), lambda h: (0, h)),
        out_shape=jax.ShapeDtypeStruct((t, heads * vdim), BF16),
        compiler_params=_params("parallel"),
        name="sample_uv",
    )(o_lat, wuv)


def _rot_cols(w, half):
    return jnp.concatenate([-w[..., half:], w[..., :half]], axis=-1)


def _rope_tables(pos, rope):
    inv_freq = ROPE_BASE ** (-jnp.arange(0, rope, 2, dtype=F32) / rope)
    ang = pos.astype(F32)[:, None] * inv_freq[None, :]
    cos, sin = jnp.cos(ang), jnp.sin(ang)
    return jnp.concatenate([cos, cos], axis=-1), jnp.concatenate([sin, sin], axis=-1)


def kernel(x_prompt, x_sample, cache_kv_latent, cache_k_rope, state_conv, state_ffn_conv, page_table, norm_mix_pre, norm_mix_post, norm_ffn_pre, norm_ffn_post, cv_w_in, cv_b_in, cv_dw, cv_dw_b, cv_ln_g, cv_ln_b, cv_w_out, cv_b_out, mla_w_in, mla_q_norm, mla_kv_norm, mla_w_qb, mla_w_uk, mla_w_uv, mla_w_o, ffn_w_gate, ffn_w_up, ffn_dw, ffn_dw_b, ffn_w_down):
    batch, seq, d = x_prompt.shape
    n_dec, l_dec, _ = x_sample.shape
    depth = norm_mix_pre.shape[0]
    conv_w = cv_dw.shape[1]
    ffn_w = ffn_dw.shape[1]
    heads, nope = mla_w_uk.shape[2], mla_w_uk.shape[3]
    q_lora, kv_lora = mla_q_norm.shape[1], mla_kv_norm.shape[1]
    rope = mla_w_in.shape[2] - q_lora - kv_lora
    vdim = mla_w_uv.shape[3]
    dff = ffn_w_gate.shape[2]
    past = page_table.shape[1] * cache_kv_latent.shape[2]
    scale = 1.0 / math.sqrt(nope + rope)
    tp = batch * seq
    ts = n_dec * l_dec

    tm = min(512, seq)
    tm_ffn = min(1024, seq)
    tf = 512 if dff % 512 == 0 else dff
    tn = min(512, d)
    tm_conv = min(256, seq)
    pages = min(32, page_table.shape[1])
    flash_heads = 2

    vec = lambda a: a[:, None, :]
    n_pre, n_post, f_pre, f_post = vec(norm_mix_pre), vec(norm_mix_post), vec(norm_ffn_pre), vec(norm_ffn_post)
    bf = lambda a: a.astype(BF16)
    cv_w_in_b, cv_w_out_b = bf(cv_w_in), bf(cv_w_out)
    w_gate_b, w_up_b, w_down_b = bf(ffn_w_gate), bf(ffn_w_up), bf(ffn_w_down)

    cache_rt = cache_k_rope.transpose(0, 1, 3, 2)

    cos_p, sin_p = _rope_tables(jnp.tile(jnp.arange(seq), batch), rope)
    cos_s, sin_s = _rope_tables(jnp.repeat(past + jnp.arange(l_dec), n_dec), rope)

    xp = x_prompt.reshape(tp, d)
    kv_p, kr_p, conv_p, ffn_p = [], [], [], []
    xs = x_sample.transpose(1, 0, 2).reshape(ts, d)
    kv_s, kr_s, conv_s, ffn_s = [], [], [], []

    for i in range(depth):
        j = i // 2
        if i % 2 == 0:
            b_in = cv_b_in[j][None, :]
            args = (cv_dw[j], cv_dw_b[j][None, :], cv_ln_g[j][None, :], cv_ln_b[j][None, :], cv_w_out_b,
                    cv_b_out[j][None, :], n_post[i])
            u = conv_in(xp, n_pre[i], cv_w_in_b, b_in, layer=j, tm=tm, tn=tn)
            conv_p.append(u.reshape(batch, seq, d)[:, seq - (conv_w - 1):])
            xp = conv_out_prompt(u, xp, *args, layer=j, seq=seq, tm=tm_conv)

            us = conv_in(xs, n_pre[i], cv_w_in_b, b_in, layer=j, tm=ts, tn=tn)
            u_ext = jnp.concatenate([state_conv[j].transpose(1, 0, 2), us.reshape(l_dec, n_dec, d)], axis=0)
            conv_s.append(u_ext[l_dec:].transpose(1, 0, 2))
            xs = conv_out_sample(u_ext, xs.reshape(l_dec, n_dec, d), *args, layer=j, ns=32).reshape(ts, d)
        else:
            w_in = mla_w_in[j]
            w_in_ext = bf(jnp.concatenate([w_in, _rot_cols(w_in[:, q_lora + kv_lora:], rope // 2)], axis=-1))
            w_qb = mla_w_qb[j]
            wq = bf(jnp.concatenate([w_qb, _rot_cols(w_qb[..., nope:], rope // 2)], axis=-1))
            wq = wq.reshape(q_lora, heads * (nope + 2 * rope))
            wuk = bf(mla_w_uk[j]).reshape(kv_lora, heads * nope)
            wuv = bf(mla_w_uv[j]).reshape(kv_lora, heads * vdim)
            w_o = bf(mla_w_o[j]).reshape(heads * vdim, d)
            w = (w_in_ext, mla_q_norm[j][None, :], mla_kv_norm[j][None, :])

            qa, ckv, kr = mla_in(xp, n_pre[i], *w, cos_p, sin_p, tm=tm)
            kv_p.append(ckv.reshape(batch, seq, kv_lora))
            kr_p.append(kr.reshape(batch, seq, rope))
            q, k, v = mla_qkv(qa, ckv, kr, cos_p, sin_p, wq, wuk, wuv, heads=heads, tm=tm, scale=scale)
            o = flash_prompt(q.reshape(batch, seq, -1), k.reshape(batch, seq, -1), v.reshape(batch, seq, -1),
                             heads=heads, tq=tm, hb=flash_heads)
            xp = proj_post(o.reshape(tp, heads * vdim), xp, w_o, n_post[i], tm=tm)

            qa, ckv, kr = mla_in(xs, n_pre[i], *w, cos_s, sin_s, tm=ts)
            c_new = ckv.reshape(l_dec, n_dec, kv_lora).transpose(1, 0, 2)
            r_new = kr.reshape(l_dec, n_dec, rope).transpose(1, 0, 2)
            kv_s.append(c_new)
            kr_s.append(r_new)
            qs = sample_q(qa, cos_s, sin_s, wq, wuk, heads=heads, scale=scale)
            qs = qs.reshape(heads, l_dec, n_dec, kv_lora + rope).transpose(2, 1, 0, 3)
            qs = qs.reshape(n_dec, l_dec * heads, kv_lora + rope)
            o_lat = decode_attention(page_table, qs, c_new, r_new, cache_kv_latent, cache_rt,
                                     layer=j, pages=pages, heads=heads)
            o_lat = o_lat.reshape(n_dec, l_dec, heads, kv_lora).transpose(2, 1, 0, 3).reshape(heads, ts, kv_lora)
            xs = proj_post(sample_uv(o_lat, wuv), xs, w_o, n_post[i], tm=ts)

        ffn_w_args = (w_gate_b, w_up_b, ffn_dw[i], ffn_dw_b[i][None, :], w_down_b)
        xp, g_tail = ffn_prompt(xp, f_pre[i], f_post[i], *ffn_w_args, layer=i, seq=seq, tm=tm_ffn, tf=tf)
        g_tail = g_tail.reshape(batch, seq // tm_ffn, FFN_TAIL, dff)
        ffn_p.append(g_tail[:, -1, FFN_TAIL - (ffn_w - 1):])
        xs, g_new = ffn_sample(xs, f_pre[i], f_post[i], state_ffn_conv[i].transpose(1, 0, 2), *ffn_w_args,
                               layer=i, tf=tf)
        ffn_s.append(g_new.transpose(1, 0, 2))

    y_prompt = xp.reshape(batch, seq, d)
    y_sample = xs.reshape(l_dec, n_dec, d).transpose(1, 0, 2)
    return (y_prompt, y_sample, jnp.stack(kv_p), jnp.stack(kr_p), jnp.stack(kv_s), jnp.stack(kr_s),
            jnp.stack(conv_p), jnp.stack(conv_s), jnp.stack(ffn_p), jnp.stack(ffn_s))
```

```python
import functools
import math

import jax
import jax.numpy as jnp
from jax import lax
from jax.experimental import pallas as pl
from jax.experimental.pallas import tpu as pltpu

EPS = 1e-6
ROPE_BASE = 10000.0
F32 = jnp.float32
BF16 = jnp.bfloat16

V7X_VMEM_LIMIT_BYTES = 56 * 1024 * 1024


def _params(*sem):
    return pltpu.CompilerParams(dimension_semantics=sem, vmem_limit_bytes=V7X_VMEM_LIMIT_BYTES)


def _dot(a, b):
    return jnp.dot(a, b, preferred_element_type=F32)


def _dot_nt(a, b):
    return lax.dot_general(a, b, (((1,), (1,)), ((), ())), preferred_element_type=F32)


def _rms(xf, g):
    ms = jnp.mean(xf * xf, axis=-1, keepdims=True)
    return (xf * lax.rsqrt(ms + EPS)) * g


def _silu(x):
    return x * jax.nn.sigmoid(x)


def _conv_in_kernel(x_ref, g_ref, w1_ref, w2_ref, b1_ref, b2_ref, u_ref, h_ref):
    @pl.when(pl.program_id(1) == 0)
    def _():
        h_ref[...] = _rms(x_ref[...], g_ref[...]).astype(BF16)

    h = h_ref[...]
    a1 = _dot(h, w1_ref[...]) + b1_ref[...]
    a2 = _dot(h, w2_ref[...]) + b2_ref[...]
    u_ref[...] = a1 * jax.nn.sigmoid(a2)


def conv_in(x, g_pre, w_in, b_in, *, layer, tm, tn):
    t, d = x.shape
    nj = d // tn
    return pl.pallas_call(
        _conv_in_kernel,
        grid=(t // tm, nj),
        in_specs=[
            pl.BlockSpec((tm, d), lambda i, j: (i, 0)),
            pl.BlockSpec((1, d), lambda i, j: (0, 0)),
            pl.BlockSpec((None, d, tn), lambda i, j: (layer, 0, j)),
            pl.BlockSpec((None, d, tn), lambda i, j: (layer, 0, j + nj)),
            pl.BlockSpec((1, tn), lambda i, j: (0, j)),
            pl.BlockSpec((1, tn), lambda i, j: (0, j + nj)),
        ],
        out_specs=pl.BlockSpec((tm, tn), lambda i, j: (i, j)),
        out_shape=jax.ShapeDtypeStruct((t, d), F32),
        scratch_shapes=[pltpu.VMEM((tm, d), BF16)],
        compiler_params=_params("parallel", "arbitrary"),
        name="conv_in",
    )(x, g_pre, w_in, w_in, b_in, b_in)


SUBLANES = 8
CONV_HALO = 32


def _conv_out_prompt_kernel(u_ref, uh_ref, x_ref, dw_ref, dwb_ref, lng_ref, lnb_ref, w_ref, b_ref, gp_ref,
                            o_ref, ext_ref, c_ref, sh_ref, *, tiles_per_seq, width, lane_chunk):
    tm, d = u_ref.shape
    first = pl.program_id(0) % tiles_per_seq == 0
    ext_ref[0:CONV_HALO, :] = jnp.where(first, 0.0, uh_ref[...])
    ext_ref[CONV_HALO:, :] = u_ref[...]
    base = CONV_HALO - (width - 1)
    n_sh = sh_ref.shape[1]

    def chunk(c, carry):
        col = pl.ds(pl.multiple_of(c * lane_chunk, lane_chunk), lane_chunk)
        for b in range(1, SUBLANES):
            sh_ref[b - 1] = ext_ref[pl.ds(b, n_sh), col]
        acc = jnp.zeros((tm, lane_chunk), F32)
        for k in range(width):
            b = (base + k) % SUBLANES
            a8 = base + k - b
            tap = ext_ref[pl.ds(a8, tm), col] if b == 0 else sh_ref[b - 1, pl.ds(a8, tm), :]
            acc = acc + tap * dw_ref[pl.ds(k, 1), col]
        c_ref[:, col] = acc + dwb_ref[:, col]
        return carry

    lax.fori_loop(0, d // lane_chunk, chunk, 0)
    c = c_ref[...]
    mu = jnp.mean(c, axis=-1, keepdims=True)
    xc = c - mu
    var = jnp.mean(xc * xc, axis=-1, keepdims=True)
    y = xc * lax.rsqrt(var + EPS) * lng_ref[...] + lnb_ref[...]
    z = _dot(_silu(y).astype(BF16), w_ref[...]) + b_ref[...]
    o_ref[...] = x_ref[...] + _rms(z, gp_ref[...])


def conv_out_prompt(u, x, dw, dw_b, ln_g, ln_b, w_out, b_out, g_post, *, layer, seq, tm):
    t, d = u.shape
    width = dw.shape[0]
    hb = tm // CONV_HALO
    row = lambda i: (i, 0)
    fixed = lambda i: (0, 0)
    kern = functools.partial(_conv_out_prompt_kernel, tiles_per_seq=seq // tm, width=width, lane_chunk=128)
    return pl.pallas_call(
        kern,
        grid=(t // tm,),
        in_specs=[
            pl.BlockSpec((tm, d), row),
            pl.BlockSpec((CONV_HALO, d), lambda i: (jnp.maximum(i * hb - 1, 0), 0)),
            pl.BlockSpec((tm, d), row),
            pl.BlockSpec((width, d), fixed),
            pl.BlockSpec((1, d), fixed),
            pl.BlockSpec((1, d), fixed),
            pl.BlockSpec((1, d), fixed),
            pl.BlockSpec((None, d, d), lambda i: (layer, 0, 0)),
            pl.BlockSpec((1, d), fixed),
            pl.BlockSpec((1, d), fixed),
        ],
        out_specs=pl.BlockSpec((tm, d), row),
        out_shape=jax.ShapeDtypeStruct((t, d), F32),
        scratch_shapes=[pltpu.VMEM((tm + CONV_HALO, d), F32), pltpu.VMEM((tm, d), F32),
                        pltpu.VMEM((SUBLANES - 1, tm + CONV_HALO - SUBLANES, 128), F32)],
        compiler_params=_params("parallel"),
        name="conv_out_prompt",
    )(u, u, x, dw, dw_b, ln_g, ln_b, w_out, b_out, g_post)


def _conv_out_sample_kernel(ue_ref, x_ref, dw_ref, dwb_ref, lng_ref, lnb_ref, w_ref, b_ref, gp_ref, o_ref,
                            *, width):
    n_tok, ns, d = x_ref.shape
    acts = []
    for t in range(n_tok):
        c = jnp.zeros((ns, d), F32)
        for k in range(width):
            c = c + ue_ref[t + k] * dw_ref[pl.ds(k, 1), :]
        c = c + dwb_ref[...]
        mu = jnp.mean(c, axis=-1, keepdims=True)
        xc = c - mu
        var = jnp.mean(xc * xc, axis=-1, keepdims=True)
        y = xc * lax.rsqrt(var + EPS) * lng_ref[...] + lnb_ref[...]
        acts.append(_silu(y).astype(BF16))
    z = _dot(jnp.concatenate(acts, axis=0), w_ref[...]) + b_ref[...]
    r = _rms(z, gp_ref[...])
    for t in range(n_tok):
        o_ref[t] = x_ref[t] + r[t * ns:(t + 1) * ns]


def conv_out_sample(u_ext, x, dw, dw_b, ln_g, ln_b, w_out, b_out, g_post, *, layer, ns):
    rows, n, d = u_ext.shape
    n_tok = x.shape[0]
    width = dw.shape[0]
    fixed = lambda i: (0, 0)
    return pl.pallas_call(
        functools.partial(_conv_out_sample_kernel, width=width),
        grid=(n // ns,),
        in_specs=[
            pl.BlockSpec((rows, ns, d), lambda i: (0, i, 0)),
            pl.BlockSpec((n_tok, ns, d), lambda i: (0, i, 0)),
            pl.BlockSpec((width, d), fixed),
            pl.BlockSpec((1, d), fixed),
            pl.BlockSpec((1, d), fixed),
            pl.BlockSpec((1, d), fixed),
            pl.BlockSpec((None, d, d), lambda i: (layer, 0, 0)),
            pl.BlockSpec((1, d), fixed),
            pl.BlockSpec((1, d), fixed),
        ],
        out_specs=pl.BlockSpec((n_tok, ns, d), lambda i: (0, i, 0)),
        out_shape=jax.ShapeDtypeStruct(x.shape, F32),
        compiler_params=_params("parallel"),
        name="conv_out_sample",
    )(u_ext, x, dw, dw_b, ln_g, ln_b, w_out, b_out, g_post)


FFN_TAIL = 8


def _ffn_finish(x_ref, gpost_ref, acc_ref, o_ref):
    o_ref[...] = x_ref[...] + _rms(acc_ref[...], gpost_ref[...])


def _ffn_prompt_kernel(x_ref, gpre_ref, gpost_ref, wg_ref, wu_ref, dw_ref, dwb_ref, wd_ref,
                       o_ref, gt_ref, h_ref, carry_ref, *, tiles_per_seq):
    acc_ref = o_ref
    i, f = pl.program_id(0), pl.program_id(1)
    tm = x_ref.shape[0]

    @pl.when(f == 0)
    def _():
        h_ref[...] = _rms(x_ref[...], gpre_ref[...]).astype(BF16)
        acc_ref[...] = jnp.zeros_like(acc_ref)

    @pl.when(i % tiles_per_seq == 0)
    def _():
        carry_ref[f] = jnp.zeros(carry_ref.shape[1:], F32)

    h = h_ref[...]
    g = _dot(h, wg_ref[...])
    u = _dot(h, wu_ref[...])
    prev = carry_ref[f]
    p1 = prev[FFN_TAIL - 1:FFN_TAIL]
    p2 = prev[FFN_TAIL - 2:FFN_TAIL - 1]
    row = lax.broadcasted_iota(jnp.int32, g.shape, 0)
    g1 = jnp.where(row == 0, p1, pltpu.roll(g, 1, 0))
    g2 = jnp.where(row == 0, p2, jnp.where(row == 1, p1, pltpu.roll(g, 2, 0)))
    gc = g2 * dw_ref[0:1, :] + g1 * dw_ref[1:2, :] + g * dw_ref[2:3, :] + dwb_ref[...]
    act = (_silu(gc) * u).astype(BF16)
    acc_ref[...] += _dot(act, wd_ref[...])
    tail = g[tm - FFN_TAIL:, :]
    carry_ref[f] = tail
    gt_ref[...] = tail

    @pl.when(f == pl.num_programs(1) - 1)
    def _():
        _ffn_finish(x_ref, gpost_ref, acc_ref, o_ref)


def ffn_prompt(x, g_pre, g_post, w_gate, w_up, dw, dw_b, w_down, *, layer, seq, tm, tf):
    t, d = x.shape
    dff = w_gate.shape[2]
    nm, nf = t // tm, dff // tf
    kern = functools.partial(_ffn_prompt_kernel, tiles_per_seq=seq // tm)
    return pl.pallas_call(
        kern,
        grid=(nm, nf),
        in_specs=[
            pl.BlockSpec((tm, d), lambda i, f: (i, 0), pipeline_mode=pl.Buffered(1)),
            pl.BlockSpec((1, d), lambda i, f: (0, 0)),
            pl.BlockSpec((1, d), lambda i, f: (0, 0)),
            pl.BlockSpec((None, d, tf), lambda i, f: (layer, 0, f)),
            pl.BlockSpec((None, d, tf), lambda i, f: (layer, 0, f)),
            pl.BlockSpec((dw.shape[0], tf), lambda i, f: (0, f)),
            pl.BlockSpec((1, tf), lambda i, f: (0, f)),
            pl.BlockSpec((None, tf, d), lambda i, f: (layer, f, 0)),
        ],
        out_specs=[
            pl.BlockSpec((tm, d), lambda i, f: (i, 0)),
            pl.BlockSpec((FFN_TAIL, tf), lambda i, f: (i, f)),
        ],
        out_shape=[jax.ShapeDtypeStruct((t, d), F32), jax.ShapeDtypeStruct((nm * FFN_TAIL, dff), F32)],
        scratch_shapes=[pltpu.VMEM((tm, d), BF16), pltpu.VMEM((nf, FFN_TAIL, tf), F32)],
        compiler_params=_params("arbitrary", "arbitrary"),
        name="ffn_prompt",
    )(x, g_pre, g_post, w_gate, w_up, dw, dw_b, w_down)


def _ffn_sample_kernel(x_ref, gpre_ref, gpost_ref, st_ref, wg_ref, wu_ref, dw_ref, dwb_ref, wd_ref,
                       o_ref, gn_ref, h_ref, acc_ref):
    f = pl.program_id(0)
    n_state, n = st_ref.shape[0], st_ref.shape[1]
    t_rows = x_ref.shape[0]

    @pl.when(f == 0)
    def _():
        h_ref[...] = _rms(x_ref[...], gpre_ref[...]).astype(BF16)
        acc_ref[...] = jnp.zeros_like(acc_ref)

    h = h_ref[...]
    g = _dot(h, wg_ref[...])
    u = _dot(h, wu_ref[...])
    s0, s1 = st_ref[0], st_ref[1]
    g1 = jnp.concatenate([s1, g[:t_rows - n]], axis=0)
    g2 = jnp.concatenate([s0, s1, g[:t_rows - 2 * n]], axis=0)
    gc = g2 * dw_ref[0:1, :] + g1 * dw_ref[1:2, :] + g * dw_ref[2:3, :] + dwb_ref[...]
    act = (_silu(gc) * u).astype(BF16)
    acc_ref[...] += _dot(act, wd_ref[...])
    for s in range(n_state):
        gn_ref[s] = g[t_rows - (n_state - s) * n:t_rows - (n_state - s - 1) * n]

    @pl.when(f == pl.num_programs(0) - 1)
    def _():
        _ffn_finish(x_ref, gpost_ref, acc_ref, o_ref)


def ffn_sample(x, g_pre, g_post, state, w_gate, w_up, dw, dw_b, w_down, *, layer, tf):
    t, d = x.shape
    dff = w_gate.shape[2]
    n_state, n, _ = state.shape
    assert dw.shape[0] == 3 and n_state == 2 and t >= 2 * n
    return pl.pallas_call(
        _ffn_sample_kernel,
        grid=(dff // tf,),
        in_specs=[
            pl.BlockSpec((t, d), lambda f: (0, 0)),
            pl.BlockSpec((1, d), lambda f: (0, 0)),
            pl.BlockSpec((1, d), lambda f: (0, 0)),
            pl.BlockSpec((n_state, n, tf), lambda f: (0, 0, f)),
            pl.BlockSpec((None, d, tf), lambda f: (layer, 0, f)),
            pl.BlockSpec((None, d, tf), lambda f: (layer, 0, f)),
            pl.BlockSpec((3, tf), lambda f: (0, f)),
            pl.BlockSpec((1, tf), lambda f: (0, f)),
            pl.BlockSpec((None, tf, d), lambda f: (layer, f, 0)),
        ],
        out_specs=[
            pl.BlockSpec((t, d), lambda f: (0, 0)),
            pl.BlockSpec((n_state, n, tf), lambda f: (0, 0, f)),
        ],
        out_shape=[jax.ShapeDtypeStruct((t, d), F32), jax.ShapeDtypeStruct(state.shape, F32)],
        scratch_shapes=[pltpu.VMEM((t, d), BF16), pltpu.VMEM((t, d), F32)],
        compiler_params=_params("arbitrary"),
        name="ffn_sample",
    )(x, g_pre, g_post, state, w_gate, w_up, dw, dw_b, w_down)


def _mla_in_kernel(x_ref, g_ref, w_ref, qn_ref, kvn_ref, cos_ref, sin_ref, qa_ref, ckv_ref, kr_ref,
                   *, q_lora, kv_lora, rope):
    h = _rms(x_ref[...], g_ref[...]).astype(BF16)
    z = _dot(h, w_ref[...])
    qa_ref[...] = _rms(z[:, :q_lora], qn_ref[...]).astype(BF16)
    ckv_ref[...] = _rms(z[:, q_lora:q_lora + kv_lora], kvn_ref[...])
    o = q_lora + kv_lora
    kr_ref[...] = z[:, o:o + rope] * cos_ref[...] + z[:, o + rope:o + 2 * rope] * sin_ref[...]


def mla_in(x, g_pre, w_ext, q_norm, kv_norm, cos, sin, *, tm):
    t, d = x.shape
    q_lora, kv_lora, rope = q_norm.shape[1], kv_norm.shape[1], cos.shape[1]
    n_ext = w_ext.shape[1]
    row = lambda i: (i, 0)
    fixed = lambda i: (0, 0)
    kern = functools.partial(_mla_in_kernel, q_lora=q_lora, kv_lora=kv_lora, rope=rope)
    return pl.pallas_call(
        kern,
        grid=(t // tm,),
        in_specs=[
            pl.BlockSpec((tm, d), row),
            pl.BlockSpec((1, d), fixed),
            pl.BlockSpec((d, n_ext), fixed),
            pl.BlockSpec((1, q_lora), fixed),
            pl.BlockSpec((1, kv_lora), fixed),
            pl.BlockSpec((tm, rope), row),
            pl.BlockSpec((tm, rope), row),
        ],
        out_specs=[pl.BlockSpec((tm, q_lora), row), pl.BlockSpec((tm, kv_lora), row),
                   pl.BlockSpec((tm, rope), row)],
        out_shape=[jax.ShapeDtypeStruct((t, q_lora), BF16), jax.ShapeDtypeStruct((t, kv_lora), F32),
                   jax.ShapeDtypeStruct((t, rope), F32)],
        compiler_params=_params("parallel"),
        name="mla_in",
    )(x, g_pre, w_ext, q_norm, kv_norm, cos, sin)


def _mla_qkv_kernel(qa_ref, ckv_ref, kr_ref, cos_ref, sin_ref, wq_ref, wuk_ref, wuv_ref,
                    q_ref, k_ref, v_ref, *, heads, nope, rope, scale):
    tm = qa_ref.shape[0]
    hd = nope + 2 * rope
    qa = qa_ref[...]
    ckv = ckv_ref[...].astype(BF16)
    cos, sin = cos_ref[...], sin_ref[...]
    kr = kr_ref[...].astype(BF16)
    pad = jnp.zeros((tm, hd - nope - rope), BF16)
    for h in range(heads):
        q = _dot(qa, wq_ref[:, h * hd:(h + 1) * hd])
        qn = q[:, :nope] * scale
        qr = (q[:, nope:nope + rope] * cos + q[:, nope + rope:] * sin) * scale
        q_ref[:, h * hd:h * hd + nope] = qn.astype(BF16)
        q_ref[:, h * hd + nope:h * hd + nope + rope] = qr.astype(BF16)
        q_ref[:, h * hd + nope + rope:(h + 1) * hd] = pad
    for hp in range(heads // 2):
        kn2 = _dot(ckv, wuk_ref[:, 2 * hp * nope:2 * (hp + 1) * nope])
        for j in range(2):
            h = 2 * hp + j
            k_ref[:, h * hd:h * hd + nope] = kn2[:, j * nope:(j + 1) * nope].astype(BF16)
            k_ref[:, h * hd + nope:h * hd + nope + rope] = kr
            k_ref[:, h * hd + nope + rope:(h + 1) * hd] = pad
    v_ref[...] = _dot(ckv, wuv_ref[...]).astype(BF16)


def mla_qkv(qa, ckv, kr, cos, sin, wq, wuk, wuv, *, heads, tm, scale):
    t, q_lora = qa.shape
    kv_lora, rope = ckv.shape[1], kr.shape[1]
    nope = wuk.shape[1] // heads
    hd = nope + 2 * rope
    assert heads % 2 == 0 and wq.shape[1] == heads * hd
    row = lambda i: (i, 0)
    fixed = lambda i: (0, 0)
    kern = functools.partial(_mla_qkv_kernel, heads=heads, nope=nope, rope=rope, scale=scale)
    return pl.pallas_call(
        kern,
        grid=(t // tm,),
        in_specs=[
            pl.BlockSpec((tm, q_lora), row),
            pl.BlockSpec((tm, kv_lora), row),
            pl.BlockSpec((tm, rope), row),
            pl.BlockSpec((tm, rope), row),
            pl.BlockSpec((tm, rope), row),
            pl.BlockSpec(wq.shape, fixed),
            pl.BlockSpec(wuk.shape, fixed),
            pl.BlockSpec(wuv.shape, fixed),
        ],
        out_specs=[pl.BlockSpec((tm, heads * hd), row), pl.BlockSpec((tm, heads * hd), row),
                   pl.BlockSpec((tm, wuv.shape[1]), row)],
        out_shape=[jax.ShapeDtypeStruct((t, heads * hd), BF16), jax.ShapeDtypeStruct((t, heads * hd), BF16),
                   jax.ShapeDtypeStruct((t, wuv.shape[1]), BF16)],
        compiler_params=_params("parallel"),
        name="mla_qkv",
    )(qa, ckv, kr, cos, sin, wq, wuk, wuv)


def _flash_kernel(q_ref, k_ref, v_ref, o_ref, *, tk, hb):
    tq = q_ref.shape[1]
    hd = q_ref.shape[2] // hb
    vdim = v_ref.shape[2] // hb
    qi = pl.program_id(2)
    qs = [q_ref[0, :, h * hd:(h + 1) * hd] for h in range(hb)]

    n_sub = tq // tk

    def step(h, kb, carry, diag):
        m, l, acc = carry
        rows = pl.ds(pl.multiple_of(kb * tk, tk), tk)
        s = _dot_nt(qs[h], k_ref[0, rows, h * hd:(h + 1) * hd])
        if diag is not None:
            r = lax.broadcasted_iota(jnp.int32, s.shape, 0)
            c = lax.broadcasted_iota(jnp.int32, s.shape, 1) + diag * tk
            s = jnp.where(c <= r, s, -jnp.inf)
        m_new = jnp.maximum(m, jnp.max(s, axis=-1, keepdims=True))
        alpha = jnp.exp(m - m_new)
        p = jnp.exp(s - m_new)
        l = alpha * l + jnp.sum(p, axis=-1, keepdims=True)
        acc = alpha * acc + _dot(p.astype(BF16), v_ref[0, rows, h * vdim:(h + 1) * vdim])
        return m_new, l, acc

    init1 = (jnp.full((tq, 1), -jnp.inf, F32), jnp.zeros((tq, 1), F32), jnp.zeros((tq, vdim), F32))
    body = lambda kb, cs: tuple(step(h, kb, cs[h], None) for h in range(hb))
    carries = lax.fori_loop(0, qi * n_sub, body, tuple(init1 for _ in range(hb)))
    for j in range(n_sub):
        carries = tuple(step(h, qi * n_sub + j, carries[h], j) for h in range(hb))
    for h in range(hb):
        m, l, acc = carries[h]
        o_ref[0, :, h * vdim:(h + 1) * vdim] = (acc / l).astype(o_ref.dtype)


def flash_prompt(q, k, v, *, heads, tq, tk, hb):
    batch, seq, _ = q.shape
    hd = q.shape[2] // heads
    vdim = v.shape[2] // heads
    assert tq % tk == 0 and seq % tq == 0
    kern = functools.partial(_flash_kernel, tk=tk, hb=hb)
    return pl.pallas_call(
        kern,
        grid=(batch, heads // hb, seq // tq),
        in_specs=[
            pl.BlockSpec((1, tq, hb * hd), lambda n, g, i: (n, i, g)),
            pl.BlockSpec((1, seq, hb * hd), lambda n, g, i: (n, 0, g)),
            pl.BlockSpec((1, seq, hb * vdim), lambda n, g, i: (n, 0, g)),
        ],
        out_specs=pl.BlockSpec((1, tq, hb * vdim), lambda n, g, i: (n, i, g)),
        out_shape=jax.ShapeDtypeStruct((batch, seq, heads * vdim), BF16),
        compiler_params=_params("parallel", "parallel", "arbitrary"),
        name="flash_prompt",
    )(q, k, v)


def _proj_post_kernel(a_ref, x_ref, w_ref, gp_ref, o_ref):
    o_ref[...] = x_ref[...] + _rms(_dot(a_ref[...], w_ref[...]), gp_ref[...])


def proj_post(a, x, w, g_post, *, tm):
    t, kdim = a.shape
    d = x.shape[1]
    row = lambda i: (i, 0)
    fixed = lambda i: (0, 0)
    return pl.pallas_call(
        _proj_post_kernel,
        grid=(t // tm,),
        in_specs=[pl.BlockSpec((tm, kdim), row), pl.BlockSpec((tm, d), row), pl.BlockSpec((kdim, d), fixed),
                  pl.BlockSpec((1, d), fixed)],
        out_specs=pl.BlockSpec((tm, d), row),
        out_shape=jax.ShapeDtypeStruct((t, d), F32),
        compiler_params=_params("parallel"),
        name="proj_post",
    )(a, x, w, g_post)


def _sample_q_kernel(qa_ref, cos_ref, sin_ref, wq_ref, wuk_ref, o_ref, *, nope, rope, scale):
    q = _dot(qa_ref[...], wq_ref[...])
    qn = q[:, :nope].astype(BF16)
    qr = (q[:, nope:nope + rope] * cos_ref[...] + q[:, nope + rope:] * sin_ref[...]) * scale
    qlat = _dot_nt(qn, wuk_ref[...]) * scale
    o_ref[0] = jnp.concatenate([qlat, qr], axis=1).astype(BF16)


def sample_q(qa, cos, sin, wq, wuk, *, heads, scale):
    t, q_lora = qa.shape
    rope = cos.shape[1]
    kv_lora = wuk.shape[0]
    nope = wuk.shape[1] // heads
    fixed = lambda h: (0, 0)
    head = lambda h: (0, h)
    kern = functools.partial(_sample_q_kernel, nope=nope, rope=rope, scale=scale)
    return pl.pallas_call(
        kern,
        grid=(heads,),
        in_specs=[pl.BlockSpec((t, q_lora), fixed), pl.BlockSpec((t, rope), fixed),
                  pl.BlockSpec((t, rope), fixed),
                  pl.BlockSpec((q_lora, nope + 2 * rope), head), pl.BlockSpec((kv_lora, nope), head)],
        out_specs=pl.BlockSpec((1, t, kv_lora + rope), lambda h: (h, 0, 0)),
        out_shape=jax.ShapeDtypeStruct((heads, t, kv_lora + rope), BF16),
        compiler_params=_params("parallel"),
        name="sample_q",
    )(qa, cos, sin, wq, wuk)


DECODE_SLOTS = 3


def _decode_kernel(pt_ref, q_ref, cn_ref, rn_ref, cc_hbm, crt_hbm, o_ref, cbuf, rtbuf, sem, m_ref, l_ref, acc_ref,
                   *, layer, pages, heads):
    ci = pl.program_id(1)
    n_chunks = pl.num_programs(1)
    step = pl.program_id(0) * n_chunks + ci
    n_steps = pl.num_programs(0) * n_chunks
    page = cc_hbm.shape[2]
    kv_lora = cc_hbm.shape[3]

    def page_copies(s, j):
        slot = s % DECODE_SLOTS
        pg = pt_ref[s * pages + j]
        rows = pl.ds(j * page, page)
        return (pltpu.make_async_copy(cc_hbm.at[layer, pg], cbuf.at[slot, rows], sem.at[0, slot]),
                pltpu.make_async_copy(crt_hbm.at[layer, pg], rtbuf.at[slot, :, rows], sem.at[1, slot]))

    def start_step(s):
        for j in range(pages):
            for cp in page_copies(s, j):
                cp.start()

    @pl.when(step == 0)
    def _():
        for s in range(DECODE_SLOTS - 1):
            start_step(s)

    ahead = step + DECODE_SLOTS - 1

    @pl.when(ahead < n_steps)
    def _():
        start_step(ahead)

    for j in range(pages):
        for cp in page_copies(step, j):
            cp.wait()

    @pl.when(ci == 0)
    def _():
        m_ref[...] = jnp.full(m_ref.shape, -jnp.inf, F32)
        l_ref[...] = jnp.zeros(l_ref.shape, F32)
        acc_ref[...] = jnp.zeros(acc_ref.shape, F32)

    slot = step % DECODE_SLOTS
    q = q_ref[0]
    qlat, qr = q[:, :kv_lora], q[:, kv_lora:]
    kc = cbuf[slot].astype(BF16)
    s = _dot_nt(qlat, kc) + _dot(qr, rtbuf[slot].astype(BF16))
    m = m_ref[...]
    m_new = jnp.maximum(m, jnp.max(s, axis=-1, keepdims=True))
    alpha = jnp.exp(m - m_new)
    p = jnp.exp(s - m_new)
    l_ref[...] = alpha * l_ref[...] + jnp.sum(p, axis=-1, keepdims=True)
    acc_ref[...] = alpha * acc_ref[...] + _dot(p.astype(BF16), kc)
    m_ref[...] = m_new

    @pl.when(ci == pl.num_programs(1) - 1)
    def _():
        n_new = cn_ref.shape[1]
        cn = cn_ref[0].astype(BF16).astype(F32)
        rn = rn_ref[0].astype(BF16).astype(F32)
        qlf, qrf = qlat.astype(F32), qr.astype(F32)
        row = lax.broadcasted_iota(jnp.int32, (q.shape[0], 1), 0)
        cols = []
        for j in range(n_new):
            sj = (jnp.sum(qlf * cn[j:j + 1], axis=-1, keepdims=True)
                  + jnp.sum(qrf * rn[j:j + 1], axis=-1, keepdims=True))
            cols.append(jnp.where(row >= j * heads, sj, -jnp.inf))
        m0 = m_ref[...]
        m1 = m0
        for sj in cols:
            m1 = jnp.maximum(m1, sj)
        a = jnp.exp(m0 - m1)
        l1 = a * l_ref[...]
        acc1 = a * acc_ref[...]
        for j, sj in enumerate(cols):
            pj = jnp.exp(sj - m1)
            l1 = l1 + pj
            acc1 = acc1 + pj.astype(BF16).astype(F32) * cn[j:j + 1]
        o_ref[0] = acc1 / l1


def decode_attention(page_table, qs, c_new, r_new, cache_c, cache_rt, *, layer, pages, heads):
    n, rows, dq = qs.shape
    n_pages = page_table.shape[1]
    _, _, page, kv_lora = cache_c.shape
    rope = cache_rt.shape[2]
    n_new = c_new.shape[1]
    n_chunks = n_pages // pages
    assert n_pages % pages == 0 and n * n_chunks >= DECODE_SLOTS - 1
    pt = page_table.reshape(-1)
    seq = lambda i, c, pt_ref: (i, 0, 0)
    grid_spec = pltpu.PrefetchScalarGridSpec(
        num_scalar_prefetch=1,
        grid=(n, n_chunks),
        in_specs=[pl.BlockSpec((1, rows, dq), seq), pl.BlockSpec((1, n_new, kv_lora), seq),
                  pl.BlockSpec((1, n_new, rope), seq),
                  pl.BlockSpec(memory_space=pl.ANY), pl.BlockSpec(memory_space=pl.ANY)],
        out_specs=pl.BlockSpec((1, rows, kv_lora), seq),
        scratch_shapes=[pltpu.VMEM((DECODE_SLOTS, pages * page, kv_lora), F32),
                        pltpu.VMEM((DECODE_SLOTS, rope, pages * page), F32),
                        pltpu.SemaphoreType.DMA((2, DECODE_SLOTS)),
                        pltpu.VMEM((rows, 1), F32), pltpu.VMEM((rows, 1), F32),
                        pltpu.VMEM((rows, kv_lora), F32)],
    )
    return pl.pallas_call(
        functools.partial(_decode_kernel, layer=layer, pages=pages, heads=heads),
        grid_spec=grid_spec,
        out_shape=jax.ShapeDtypeStruct((n, rows, kv_lora), F32),
        compiler_params=_params("arbitrary", "arbitrary"),
        name="decode_attention",
    )(pt, qs, c_new, r_new, cache_c, cache_rt)


def _sample_uv_kernel(ol_ref, wuv_ref, o_ref):
    o_ref[...] = _dot(ol_ref[0].astype(BF16), wuv_ref[...]).astype(o_ref.dtype)


def sample_uv(o_lat, wuv):
    heads, t, kv_lora = o_lat.shape
    vdim = wuv.shape[1] // heads
    return pl.pallas_call(
        _sample_uv_kernel,
        grid=(heads,),
        in_specs=[pl.BlockSpec((1, t, kv_lora), lambda h: (h, 0, 0)), pl.BlockSpec((kv_lora, vdim), lambda h: (0, h))],
        out_specs=pl.BlockSpec((t, vdim), lambda h: (0, h)),
        out_shape=jax.ShapeDtypeStruct((t, heads * vdim), BF16),
        compiler_params=_params("parallel"),
        name="sample_uv",
    )(o_lat, wuv)


def _rot_cols(w, half):
    return jnp.concatenate([-w[..., half:], w[..., :half]], axis=-1)


def _rope_tables(pos, rope):
    inv_freq = ROPE_BASE ** (-jnp.arange(0, rope, 2, dtype=F32) / rope)
    ang = pos.astype(F32)[:, None] * inv_freq[None, :]
    cos, sin = jnp.cos(ang), jnp.sin(ang)
    return jnp.concatenate([cos, cos], axis=-1), jnp.concatenate([sin, sin], axis=-1)


def kernel(x_prompt, x_sample, cache_kv_latent, cache_k_rope, state_conv, state_ffn_conv, page_table, norm_mix_pre, norm_mix_post, norm_ffn_pre, norm_ffn_post, cv_w_in, cv_b_in, cv_dw, cv_dw_b, cv_ln_g, cv_ln_b, cv_w_out, cv_b_out, mla_w_in, mla_q_norm, mla_kv_norm, mla_w_qb, mla_w_uk, mla_w_uv, mla_w_o, ffn_w_gate, ffn_w_up, ffn_dw, ffn_dw_b, ffn_w_down):
    batch, seq, d = x_prompt.shape
    n_dec, l_dec, _ = x_sample.shape
    depth = norm_mix_pre.shape[0]
    conv_w = cv_dw.shape[1]
    ffn_w = ffn_dw.shape[1]
    heads, nope = mla_w_uk.shape[2], mla_w_uk.shape[3]
    q_lora, kv_lora = mla_q_norm.shape[1], mla_kv_norm.shape[1]
    rope = mla_w_in.shape[2] - q_lora - kv_lora
    vdim = mla_w_uv.shape[3]
    dff = ffn_w_gate.shape[2]
    past = page_table.shape[1] * cache_kv_latent.shape[2]
    scale = 1.0 / math.sqrt(nope + rope)
    tp = batch * seq
    ts = n_dec * l_dec

    tm = min(512, seq)
    tm_ffn = min(1024, seq)
    tf = 512 if dff % 512 == 0 else dff
    tn = min(1024, d)
    tm_cin = min(1024, seq)
    tm_conv = min(256, seq)
    pages = min(32, page_table.shape[1])
    flash_heads = 2
    tq_flash = min(1024, seq)
    tk_flash = min(1024, seq)

    vec = lambda a: a[:, None, :]
    n_pre, n_post, f_pre, f_post = vec(norm_mix_pre), vec(norm_mix_post), vec(norm_ffn_pre), vec(norm_ffn_post)
    bf = lambda a: a.astype(BF16)
    cv_w_in_b, cv_w_out_b = bf(cv_w_in), bf(cv_w_out)
    w_gate_b, w_up_b, w_down_b = bf(ffn_w_gate), bf(ffn_w_up), bf(ffn_w_down)

    cache_rt = cache_k_rope.transpose(0, 1, 3, 2)

    cos_p, sin_p = _rope_tables(jnp.tile(jnp.arange(seq), batch), rope)
    cos_s, sin_s = _rope_tables(jnp.repeat(past + jnp.arange(l_dec), n_dec), rope)

    xp = x_prompt.reshape(tp, d)
    kv_p, kr_p, conv_p, ffn_p = [], [], [], []
    xs = x_sample.transpose(1, 0, 2).reshape(ts, d)
    kv_s, kr_s, conv_s, ffn_s = [], [], [], []

    for i in range(depth):
        j = i // 2
        if i % 2 == 0:
            b_in = cv_b_in[j][None, :]
            args = (cv_dw[j], cv_dw_b[j][None, :], cv_ln_g[j][None, :], cv_ln_b[j][None, :], cv_w_out_b,
                    cv_b_out[j][None, :], n_post[i])
            u = conv_in(xp, n_pre[i], cv_w_in_b, b_in, layer=j, tm=tm_cin, tn=tn)
            conv_p.append(u.reshape(batch, seq, d)[:, seq - (conv_w - 1):])
            xp = conv_out_prompt(u, xp, *args, layer=j, seq=seq, tm=tm_conv)

            us = conv_in(xs, n_pre[i], cv_w_in_b, b_in, layer=j, tm=ts, tn=tn)
            u_ext = jnp.concatenate([state_conv[j].transpose(1, 0, 2), us.reshape(l_dec, n_dec, d)], axis=0)
            conv_s.append(u_ext[l_dec:].transpose(1, 0, 2))
            xs = conv_out_sample(u_ext, xs.reshape(l_dec, n_dec, d), *args, layer=j, ns=32).reshape(ts, d)
        else:
            w_in = mla_w_in[j]
            w_in_ext = bf(jnp.concatenate([w_in, _rot_cols(w_in[:, q_lora + kv_lora:], rope // 2)], axis=-1))
            w_qb = mla_w_qb[j]
            wq = bf(jnp.concatenate([w_qb, _rot_cols(w_qb[..., nope:], rope // 2)], axis=-1))
            wq = wq.reshape(q_lora, heads * (nope + 2 * rope))
            wuk = bf(mla_w_uk[j]).reshape(kv_lora, heads * nope)
            wuv = bf(mla_w_uv[j]).reshape(kv_lora, heads * vdim)
            w_o = bf(mla_w_o[j]).reshape(heads * vdim, d)
            w = (w_in_ext, mla_q_norm[j][None, :], mla_kv_norm[j][None, :])

            qa, ckv, kr = mla_in(xp, n_pre[i], *w, cos_p, sin_p, tm=tm)
            kv_p.append(ckv.reshape(batch, seq, kv_lora))
            kr_p.append(kr.reshape(batch, seq, rope))
            q, k, v = mla_qkv(qa, ckv, kr, cos_p, sin_p, wq, wuk, wuv, heads=heads, tm=tm, scale=scale)
            o = flash_prompt(q.reshape(batch, seq, -1), k.reshape(batch, seq, -1), v.reshape(batch, seq, -1),
                             heads=heads, tq=tq_flash, tk=tk_flash, hb=flash_heads)
            xp = proj_post(o.reshape(tp, heads * vdim), xp, w_o, n_post[i], tm=tm)

            qa, ckv, kr = mla_in(xs, n_pre[i], *w, cos_s, sin_s, tm=ts)
            c_new = ckv.reshape(l_dec, n_dec, kv_lora).transpose(1, 0, 2)
            r_new = kr.reshape(l_dec, n_dec, rope).transpose(1, 0, 2)
            kv_s.append(c_new)
            kr_s.append(r_new)
            qs = sample_q(qa, cos_s, sin_s, wq, wuk, heads=heads, scale=scale)
            qs = qs.reshape(heads, l_dec, n_dec, kv_lora + rope).transpose(2, 1, 0, 3)
            qs = qs.reshape(n_dec, l_dec * heads, kv_lora + rope)
            o_lat = decode_attention(page_table, qs, c_new, r_new, cache_kv_latent, cache_rt,
                                     layer=j, pages=pages, heads=heads)
            o_lat = o_lat.reshape(n_dec, l_dec, heads, kv_lora).transpose(2, 1, 0, 3).reshape(heads, ts, kv_lora)
            xs = proj_post(sample_uv(o_lat, wuv), xs, w_o, n_post[i], tm=ts)

        ffn_w_args = (w_gate_b, w_up_b, ffn_dw[i], ffn_dw_b[i][None, :], w_down_b)
        xp, g_tail = ffn_prompt(xp, f_pre[i], f_post[i], *ffn_w_args, layer=i, seq=seq, tm=tm_ffn, tf=tf)
        g_tail = g_tail.reshape(batch, seq // tm_ffn, FFN_TAIL, dff)
        ffn_p.append(g_tail[:, -1, FFN_TAIL - (ffn_w - 1):])
        xs, g_new = ffn_sample(xs, f_pre[i], f_post[i], state_ffn_conv[i].transpose(1, 0, 2), *ffn_w_args,
                               layer=i, tf=tf)
        ffn_s.append(g_new.transpose(1, 0, 2))

    y_prompt = xp.reshape(batch, seq, d)
    y_sample = xs.reshape(l_dec, n_dec, d).transpose(1, 0, 2)
    return (y_prompt, y_sample, jnp.stack(kv_p), jnp.stack(kr_p), jnp.stack(kv_s), jnp.stack(kr_s),
            jnp.stack(conv_p), jnp.stack(conv_s), jnp.stack(ffn_p), jnp.stack(ffn_s))
```

```python
import functools
import math

import jax
import jax.numpy as jnp
from jax import lax
from jax.experimental import pallas as pl
from jax.experimental.pallas import tpu as pltpu

EPS = 1e-6
ROPE_BASE = 10000.0
F32 = jnp.float32
BF16 = jnp.bfloat16

V7X_VMEM_LIMIT_BYTES = 56 * 1024 * 1024


def _params(*sem):
    return pltpu.CompilerParams(dimension_semantics=sem, vmem_limit_bytes=V7X_VMEM_LIMIT_BYTES)


def _dot(a, b):
    return jnp.dot(a, b, preferred_element_type=F32)


def _dot_nt(a, b):
    return lax.dot_general(a, b, (((1,), (1,)), ((), ())), preferred_element_type=F32)


def _rms(xf, g):
    ms = jnp.mean(xf * xf, axis=-1, keepdims=True)
    return (xf * lax.rsqrt(ms + EPS)) * g


def _silu(x):
    return x * jax.nn.sigmoid(x)


def _conv_in_kernel(x_ref, g_ref, w1_ref, w2_ref, b1_ref, b2_ref, u_ref, h_ref):
    @pl.when(pl.program_id(1) == 0)
    def _():
        h_ref[...] = _rms(x_ref[...], g_ref[...]).astype(BF16)

    h = h_ref[...]
    a1 = _dot(h, w1_ref[...]) + b1_ref[...]
    a2 = _dot(h, w2_ref[...]) + b2_ref[...]
    u_ref[...] = a1 * jax.nn.sigmoid(a2)


def conv_in(x, g_pre, w_in, b_in, *, layer, tm, tn):
    t, d = x.shape
    nj = d // tn
    return pl.pallas_call(
        _conv_in_kernel,
        grid=(t // tm, nj),
        in_specs=[
            pl.BlockSpec((tm, d), lambda i, j: (i, 0)),
            pl.BlockSpec((1, d), lambda i, j: (0, 0)),
            pl.BlockSpec((None, d, tn), lambda i, j: (layer, 0, j)),
            pl.BlockSpec((None, d, tn), lambda i, j: (layer, 0, j + nj)),
            pl.BlockSpec((1, tn), lambda i, j: (0, j)),
            pl.BlockSpec((1, tn), lambda i, j: (0, j + nj)),
        ],
        out_specs=pl.BlockSpec((tm, tn), lambda i, j: (i, j)),
        out_shape=jax.ShapeDtypeStruct((t, d), F32),
        scratch_shapes=[pltpu.VMEM((tm, d), BF16)],
        compiler_params=_params("parallel", "arbitrary"),
        name="conv_in",
    )(x, g_pre, w_in, w_in, b_in, b_in)


SUBLANES = 8
CONV_HALO = 32


def _conv_out_prompt_kernel(u_ref, uh_ref, x_ref, dw_ref, dwb_ref, lng_ref, lnb_ref, w_ref, b_ref, gp_ref,
                            o_ref, ext_ref, c_ref, sh_ref, *, tiles_per_seq, width, lane_chunk):
    tm, d = u_ref.shape
    first = pl.program_id(0) % tiles_per_seq == 0
    ext_ref[0:CONV_HALO, :] = jnp.where(first, 0.0, uh_ref[...])
    ext_ref[CONV_HALO:, :] = u_ref[...]
    base = CONV_HALO - (width - 1)
    n_sh = sh_ref.shape[1]

    def chunk(c, carry):
        col = pl.ds(pl.multiple_of(c * lane_chunk, lane_chunk), lane_chunk)
        for b in range(1, SUBLANES):
            sh_ref[b - 1] = ext_ref[pl.ds(b, n_sh), col]
        acc = jnp.zeros((tm, lane_chunk), F32)
        for k in range(width):
            b = (base + k) % SUBLANES
            a8 = base + k - b
            tap = ext_ref[pl.ds(a8, tm), col] if b == 0 else sh_ref[b - 1, pl.ds(a8, tm), :]
            acc = acc + tap * dw_ref[pl.ds(k, 1), col]
        c_ref[:, col] = acc + dwb_ref[:, col]
        return carry

    lax.fori_loop(0, d // lane_chunk, chunk, 0)
    c = c_ref[...]
    mu = jnp.mean(c, axis=-1, keepdims=True)
    xc = c - mu
    var = jnp.mean(xc * xc, axis=-1, keepdims=True)
    y = xc * lax.rsqrt(var + EPS) * lng_ref[...] + lnb_ref[...]
    z = _dot(_silu(y).astype(BF16), w_ref[...]) + b_ref[...]
    o_ref[...] = x_ref[...] + _rms(z, gp_ref[...])


def conv_out_prompt(u, x, dw, dw_b, ln_g, ln_b, w_out, b_out, g_post, *, layer, seq, tm):
    t, d = u.shape
    width = dw.shape[0]
    hb = tm // CONV_HALO
    row = lambda i: (i, 0)
    fixed = lambda i: (0, 0)
    kern = functools.partial(_conv_out_prompt_kernel, tiles_per_seq=seq // tm, width=width, lane_chunk=128)
    return pl.pallas_call(
        kern,
        grid=(t // tm,),
        in_specs=[
            pl.BlockSpec((tm, d), row),
            pl.BlockSpec((CONV_HALO, d), lambda i: (jnp.maximum(i * hb - 1, 0), 0)),
            pl.BlockSpec((tm, d), row),
            pl.BlockSpec((width, d), fixed),
            pl.BlockSpec((1, d), fixed),
            pl.BlockSpec((1, d), fixed),
            pl.BlockSpec((1, d), fixed),
            pl.BlockSpec((None, d, d), lambda i: (layer, 0, 0)),
            pl.BlockSpec((1, d), fixed),
            pl.BlockSpec((1, d), fixed),
        ],
        out_specs=pl.BlockSpec((tm, d), row),
        out_shape=jax.ShapeDtypeStruct((t, d), F32),
        scratch_shapes=[pltpu.VMEM((tm + CONV_HALO, d), F32), pltpu.VMEM((tm, d), F32),
                        pltpu.VMEM((SUBLANES - 1, tm + CONV_HALO - SUBLANES, 128), F32)],
        compiler_params=_params("parallel"),
        name="conv_out_prompt",
    )(u, u, x, dw, dw_b, ln_g, ln_b, w_out, b_out, g_post)


def _conv_out_sample_kernel(st_ref, un_ref, x_ref, dw_ref, dwb_ref, lng_ref, lnb_ref, w_ref, b_ref, gp_ref,
                            o_ref, ns_ref, *, width):
    n_tok, ns, d = x_ref.shape
    n_st = st_ref.shape[0]

    def slab(r):
        return st_ref[r] if r < n_st else un_ref[r - n_st]

    for r in range(n_st):
        ns_ref[r] = slab(r + n_tok)
    acts = []
    for t in range(n_tok):
        c = jnp.zeros((ns, d), F32)
        for k in range(width):
            c = c + slab(t + k) * dw_ref[pl.ds(k, 1), :]
        c = c + dwb_ref[...]
        mu = jnp.mean(c, axis=-1, keepdims=True)
        xc = c - mu
        var = jnp.mean(xc * xc, axis=-1, keepdims=True)
        y = xc * lax.rsqrt(var + EPS) * lng_ref[...] + lnb_ref[...]
        acts.append(_silu(y).astype(BF16))
    z = _dot(jnp.concatenate(acts, axis=0), w_ref[...]) + b_ref[...]
    r = _rms(z, gp_ref[...])
    for t in range(n_tok):
        o_ref[t] = x_ref[t] + r[t * ns:(t + 1) * ns]


def conv_out_sample(state, u_new, x, dw, dw_b, ln_g, ln_b, w_out, b_out, g_post, *, layer, ns):
    _, n_st, n, d = state.shape
    n_tok = x.shape[0]
    width = dw.shape[0]
    assert n_st == width - 1 and n_tok <= n_st
    fixed = lambda i: (0, 0)
    return pl.pallas_call(
        functools.partial(_conv_out_sample_kernel, width=width),
        grid=(n // ns,),
        in_specs=[
            pl.BlockSpec((None, n_st, ns, d), lambda i: (layer, 0, i, 0)),
            pl.BlockSpec((n_tok, ns, d), lambda i: (0, i, 0)),
            pl.BlockSpec((n_tok, ns, d), lambda i: (0, i, 0)),
            pl.BlockSpec((width, d), fixed),
            pl.BlockSpec((1, d), fixed),
            pl.BlockSpec((1, d), fixed),
            pl.BlockSpec((1, d), fixed),
            pl.BlockSpec((None, d, d), lambda i: (layer, 0, 0)),
            pl.BlockSpec((1, d), fixed),
            pl.BlockSpec((1, d), fixed),
        ],
        out_specs=[pl.BlockSpec((n_tok, ns, d), lambda i: (0, i, 0)),
                   pl.BlockSpec((n_st, ns, d), lambda i: (0, i, 0))],
        out_shape=[jax.ShapeDtypeStruct(x.shape, F32), jax.ShapeDtypeStruct((n_st, n, d), F32)],
        compiler_params=_params("parallel"),
        name="conv_out_sample",
    )(state, u_new, x, dw, dw_b, ln_g, ln_b, w_out, b_out, g_post)


FFN_TAIL = 8


def _ffn_finish(x_ref, gpost_ref, acc_ref, o_ref):
    o_ref[...] = x_ref[...] + _rms(acc_ref[...], gpost_ref[...])


def _ffn_prompt_kernel(x_ref, gpre_ref, gpost_ref, wg_ref, wu_ref, dw_ref, dwb_ref, wd_ref,
                       o_ref, gt_ref, h_ref, carry_ref, *, tiles_per_seq):
    acc_ref = o_ref
    i, f = pl.program_id(0), pl.program_id(1)
    tm = x_ref.shape[0]

    @pl.when(f == 0)
    def _():
        h_ref[...] = _rms(x_ref[...], gpre_ref[...]).astype(BF16)
        acc_ref[...] = jnp.zeros_like(acc_ref)

    @pl.when(i % tiles_per_seq == 0)
    def _():
        carry_ref[f] = jnp.zeros(carry_ref.shape[1:], F32)

    h = h_ref[...]
    g = _dot(h, wg_ref[...])
    u = _dot(h, wu_ref[...])
    prev = carry_ref[f]
    p1 = prev[FFN_TAIL - 1:FFN_TAIL]
    p2 = prev[FFN_TAIL - 2:FFN_TAIL - 1]
    row = lax.broadcasted_iota(jnp.int32, g.shape, 0)
    g1 = jnp.where(row == 0, p1, pltpu.roll(g, 1, 0))
    g2 = jnp.where(row == 0, p2, jnp.where(row == 1, p1, pltpu.roll(g, 2, 0)))
    gc = g2 * dw_ref[0:1, :] + g1 * dw_ref[1:2, :] + g * dw_ref[2:3, :] + dwb_ref[...]
    act = (_silu(gc) * u).astype(BF16)
    acc_ref[...] += _dot(act, wd_ref[...])
    tail = g[tm - FFN_TAIL:, :]
    carry_ref[f] = tail
    gt_ref[...] = tail

    @pl.when(f == pl.num_programs(1) - 1)
    def _():
        _ffn_finish(x_ref, gpost_ref, acc_ref, o_ref)


def ffn_prompt(x, g_pre, g_post, w_gate, w_up, dw, dw_b, w_down, *, layer, seq, tm, tf):
    t, d = x.shape
    dff = w_gate.shape[2]
    nm, nf = t // tm, dff // tf
    kern = functools.partial(_ffn_prompt_kernel, tiles_per_seq=seq // tm)
    return pl.pallas_call(
        kern,
        grid=(nm, nf),
        in_specs=[
            pl.BlockSpec((tm, d), lambda i, f: (i, 0), pipeline_mode=pl.Buffered(1)),
            pl.BlockSpec((1, d), lambda i, f: (0, 0)),
            pl.BlockSpec((1, d), lambda i, f: (0, 0)),
            pl.BlockSpec((None, d, tf), lambda i, f: (layer, 0, f)),
            pl.BlockSpec((None, d, tf), lambda i, f: (layer, 0, f)),
            pl.BlockSpec((dw.shape[0], tf), lambda i, f: (0, f)),
            pl.BlockSpec((1, tf), lambda i, f: (0, f)),
            pl.BlockSpec((None, tf, d), lambda i, f: (layer, f, 0)),
        ],
        out_specs=[
            pl.BlockSpec((tm, d), lambda i, f: (i, 0)),
            pl.BlockSpec((FFN_TAIL, tf), lambda i, f: (i, f)),
        ],
        out_shape=[jax.ShapeDtypeStruct((t, d), F32), jax.ShapeDtypeStruct((nm * FFN_TAIL, dff), F32)],
        scratch_shapes=[pltpu.VMEM((tm, d), BF16), pltpu.VMEM((nf, FFN_TAIL, tf), F32)],
        compiler_params=_params("arbitrary", "arbitrary"),
        name="ffn_prompt",
    )(x, g_pre, g_post, w_gate, w_up, dw, dw_b, w_down)


def _ffn_sample_kernel(x_ref, gpre_ref, gpost_ref, st_ref, wg_ref, wu_ref, dw_ref, dwb_ref, wd_ref,
                       o_ref, gn_ref, h_ref, acc_ref):
    f = pl.program_id(0)
    n_state, n = st_ref.shape[0], st_ref.shape[1]
    t_rows = x_ref.shape[0]

    @pl.when(f == 0)
    def _():
        h_ref[...] = _rms(x_ref[...], gpre_ref[...]).astype(BF16)
        acc_ref[...] = jnp.zeros_like(acc_ref)

    h = h_ref[...]
    g = _dot(h, wg_ref[...])
    u = _dot(h, wu_ref[...])
    s0, s1 = st_ref[0], st_ref[1]
    g1 = jnp.concatenate([s1, g[:t_rows - n]], axis=0)
    g2 = jnp.concatenate([s0, s1, g[:t_rows - 2 * n]], axis=0)
    gc = g2 * dw_ref[0:1, :] + g1 * dw_ref[1:2, :] + g * dw_ref[2:3, :] + dwb_ref[...]
    act = (_silu(gc) * u).astype(BF16)
    acc_ref[...] += _dot(act, wd_ref[...])
    for s in range(n_state):
        gn_ref[s] = g[t_rows - (n_state - s) * n:t_rows - (n_state - s - 1) * n]

    @pl.when(f == pl.num_programs(0) - 1)
    def _():
        _ffn_finish(x_ref, gpost_ref, acc_ref, o_ref)


def ffn_sample(x, g_pre, g_post, state, w_gate, w_up, dw, dw_b, w_down, *, layer, tf):
    t, d = x.shape
    dff = w_gate.shape[2]
    _, n_state, n, _ = state.shape
    assert dw.shape[0] == 3 and n_state == 2 and t >= 2 * n
    return pl.pallas_call(
        _ffn_sample_kernel,
        grid=(dff // tf,),
        in_specs=[
            pl.BlockSpec((t, d), lambda f: (0, 0)),
            pl.BlockSpec((1, d), lambda f: (0, 0)),
            pl.BlockSpec((1, d), lambda f: (0, 0)),
            pl.BlockSpec((None, n_state, n, tf), lambda f: (layer, 0, 0, f)),
            pl.BlockSpec((None, d, tf), lambda f: (layer, 0, f)),
            pl.BlockSpec((None, d, tf), lambda f: (layer, 0, f)),
            pl.BlockSpec((3, tf), lambda f: (0, f)),
            pl.BlockSpec((1, tf), lambda f: (0, f)),
            pl.BlockSpec((None, tf, d), lambda f: (layer, f, 0)),
        ],
        out_specs=[
            pl.BlockSpec((t, d), lambda f: (0, 0)),
            pl.BlockSpec((n_state, n, tf), lambda f: (0, 0, f)),
        ],
        out_shape=[jax.ShapeDtypeStruct((t, d), F32), jax.ShapeDtypeStruct(state.shape[1:], F32)],
        scratch_shapes=[pltpu.VMEM((t, d), BF16), pltpu.VMEM((t, d), F32)],
        compiler_params=_params("arbitrary"),
        name="ffn_sample",
    )(x, g_pre, g_post, state, w_gate, w_up, dw, dw_b, w_down)


def _mla_in_kernel(x_ref, g_ref, w_ref, qn_ref, kvn_ref, cos_ref, sin_ref, qa_ref, ckv_ref, kr_ref,
                   *, q_lora, kv_lora, rope):
    h = _rms(x_ref[...], g_ref[...]).astype(BF16)
    z = _dot(h, w_ref[...])
    qa_ref[...] = _rms(z[:, :q_lora], qn_ref[...]).astype(BF16)
    ckv_ref[...] = _rms(z[:, q_lora:q_lora + kv_lora], kvn_ref[...])
    o = q_lora + kv_lora
    kr_ref[...] = z[:, o:o + rope] * cos_ref[...] + z[:, o + rope:o + 2 * rope] * sin_ref[...]


def mla_in(x, g_pre, w_ext, q_norm, kv_norm, cos, sin, *, tm):
    t, d = x.shape
    q_lora, kv_lora, rope = q_norm.shape[1], kv_norm.shape[1], cos.shape[1]
    n_ext = w_ext.shape[1]
    row = lambda i: (i, 0)
    fixed = lambda i: (0, 0)
    kern = functools.partial(_mla_in_kernel, q_lora=q_lora, kv_lora=kv_lora, rope=rope)
    return pl.pallas_call(
        kern,
        grid=(t // tm,),
        in_specs=[
            pl.BlockSpec((tm, d), row),
            pl.BlockSpec((1, d), fixed),
            pl.BlockSpec((d, n_ext), fixed),
            pl.BlockSpec((1, q_lora), fixed),
            pl.BlockSpec((1, kv_lora), fixed),
            pl.BlockSpec((tm, rope), row),
            pl.BlockSpec((tm, rope), row),
        ],
        out_specs=[pl.BlockSpec((tm, q_lora), row), pl.BlockSpec((tm, kv_lora), row),
                   pl.BlockSpec((tm, rope), row)],
        out_shape=[jax.ShapeDtypeStruct((t, q_lora), BF16), jax.ShapeDtypeStruct((t, kv_lora), F32),
                   jax.ShapeDtypeStruct((t, rope), F32)],
        compiler_params=_params("parallel"),
        name="mla_in",
    )(x, g_pre, w_ext, q_norm, kv_norm, cos, sin)


def _mla_qkv_kernel(qa_ref, ckv_ref, kr_ref, cos_ref, sin_ref, wq_ref, wuk_ref, wuv_ref,
                    q_ref, k_ref, v_ref, *, heads, nope, rope, scale):
    tm = qa_ref.shape[0]
    hd = nope + 2 * rope
    qa = qa_ref[...]
    ckv = ckv_ref[...].astype(BF16)
    cos, sin = cos_ref[...], sin_ref[...]
    kr = kr_ref[...].astype(BF16)
    pad = jnp.zeros((tm, hd - nope - rope), BF16)
    for h in range(heads):
        q = _dot(qa, wq_ref[:, h * hd:(h + 1) * hd])
        qn = q[:, :nope] * scale
        qr = (q[:, nope:nope + rope] * cos + q[:, nope + rope:] * sin) * scale
        q_ref[:, h * hd:h * hd + nope] = qn.astype(BF16)
        q_ref[:, h * hd + nope:h * hd + nope + rope] = qr.astype(BF16)
        q_ref[:, h * hd + nope + rope:(h + 1) * hd] = pad
    for hp in range(heads // 2):
        kn2 = _dot(ckv, wuk_ref[:, 2 * hp * nope:2 * (hp + 1) * nope])
        for j in range(2):
            h = 2 * hp + j
            k_ref[:, h * hd:h * hd + nope] = kn2[:, j * nope:(j + 1) * nope].astype(BF16)
            k_ref[:, h * hd + nope:h * hd + nope + rope] = kr
            k_ref[:, h * hd + nope + rope:(h + 1) * hd] = pad
    v_ref[...] = _dot(ckv, wuv_ref[...]).astype(BF16)


def mla_qkv(qa, ckv, kr, cos, sin, wq, wuk, wuv, *, heads, tm, scale):
    t, q_lora = qa.shape
    kv_lora, rope = ckv.shape[1], kr.shape[1]
    nope = wuk.shape[1] // heads
    hd = nope + 2 * rope
    assert heads % 2 == 0 and wq.shape[1] == heads * hd
    row = lambda i: (i, 0)
    fixed = lambda i: (0, 0)
    kern = functools.partial(_mla_qkv_kernel, heads=heads, nope=nope, rope=rope, scale=scale)
    return pl.pallas_call(
        kern,
        grid=(t // tm,),
        in_specs=[
            pl.BlockSpec((tm, q_lora), row),
            pl.BlockSpec((tm, kv_lora), row),
            pl.BlockSpec((tm, rope), row),
            pl.BlockSpec((tm, rope), row),
            pl.BlockSpec((tm, rope), row),
            pl.BlockSpec(wq.shape, fixed),
            pl.BlockSpec(wuk.shape, fixed),
            pl.BlockSpec(wuv.shape, fixed),
        ],
        out_specs=[pl.BlockSpec((tm, heads * hd), row), pl.BlockSpec((tm, heads * hd), row),
                   pl.BlockSpec((tm, wuv.shape[1]), row)],
        out_shape=[jax.ShapeDtypeStruct((t, heads * hd), BF16), jax.ShapeDtypeStruct((t, heads * hd), BF16),
                   jax.ShapeDtypeStruct((t, wuv.shape[1]), BF16)],
        compiler_params=_params("parallel"),
        name="mla_qkv",
    )(qa, ckv, kr, cos, sin, wq, wuk, wuv)


def _flash_kernel(q_ref, k_ref, v_ref, o_ref, *, tk, hb):
    tq = q_ref.shape[1]
    hd = q_ref.shape[2] // hb
    vdim = v_ref.shape[2] // hb
    qi = pl.program_id(2)
    qs = [q_ref[0, :, h * hd:(h + 1) * hd] for h in range(hb)]

    n_sub = tq // tk

    def step(h, kb, carry, diag):
        m, l, acc = carry
        rows = pl.ds(pl.multiple_of(kb * tk, tk), tk)
        s = _dot_nt(qs[h], k_ref[0, rows, h * hd:(h + 1) * hd])
        if diag is not None:
            r = lax.broadcasted_iota(jnp.int32, s.shape, 0)
            c = lax.broadcasted_iota(jnp.int32, s.shape, 1) + diag * tk
            s = jnp.where(c <= r, s, -jnp.inf)
        m_new = jnp.maximum(m, jnp.max(s, axis=-1, keepdims=True))
        alpha = jnp.exp(m - m_new)
        p = jnp.exp(s - m_new)
        l = alpha * l + jnp.sum(p, axis=-1, keepdims=True)
        acc = alpha * acc + _dot(p.astype(BF16), v_ref[0, rows, h * vdim:(h + 1) * vdim])
        return m_new, l, acc

    init1 = (jnp.full((tq, 1), -jnp.inf, F32), jnp.zeros((tq, 1), F32), jnp.zeros((tq, vdim), F32))
    body = lambda kb, cs: tuple(step(h, kb, cs[h], None) for h in range(hb))
    carries = lax.fori_loop(0, qi * n_sub, body, tuple(init1 for _ in range(hb)))
    for j in range(n_sub):
        carries = tuple(step(h, qi * n_sub + j, carries[h], j) for h in range(hb))
    for h in range(hb):
        m, l, acc = carries[h]
        o_ref[0, :, h * vdim:(h + 1) * vdim] = (acc / l).astype(o_ref.dtype)


def flash_prompt(q, k, v, *, heads, tq, tk, hb):
    batch, seq, _ = q.shape
    hd = q.shape[2] // heads
    vdim = v.shape[2] // heads
    assert tq % tk == 0 and seq % tq == 0
    kern = functools.partial(_flash_kernel, tk=tk, hb=hb)
    return pl.pallas_call(
        kern,
        grid=(batch, heads // hb, seq // tq),
        in_specs=[
            pl.BlockSpec((1, tq, hb * hd), lambda n, g, i: (n, i, g)),
            pl.BlockSpec((1, seq, hb * hd), lambda n, g, i: (n, 0, g)),
            pl.BlockSpec((1, seq, hb * vdim), lambda n, g, i: (n, 0, g)),
        ],
        out_specs=pl.BlockSpec((1, tq, hb * vdim), lambda n, g, i: (n, i, g)),
        out_shape=jax.ShapeDtypeStruct((batch, seq, heads * vdim), BF16),
        compiler_params=_params("parallel", "parallel", "arbitrary"),
        name="flash_prompt",
    )(q, k, v)


def _proj_post_kernel(a_ref, x_ref, w_ref, gp_ref, o_ref):
    o_ref[...] = x_ref[...] + _rms(_dot(a_ref[...], w_ref[...]), gp_ref[...])


def proj_post(a, x, w, g_post, *, tm):
    t, kdim = a.shape
    d = x.shape[1]
    row = lambda i: (i, 0)
    fixed = lambda i: (0, 0)
    return pl.pallas_call(
        _proj_post_kernel,
        grid=(t // tm,),
        in_specs=[pl.BlockSpec((tm, kdim), row), pl.BlockSpec((tm, d), row), pl.BlockSpec((kdim, d), fixed),
                  pl.BlockSpec((1, d), fixed)],
        out_specs=pl.BlockSpec((tm, d), row),
        out_shape=jax.ShapeDtypeStruct((t, d), F32),
        compiler_params=_params("parallel"),
        name="proj_post",
    )(a, x, w, g_post)


def _sample_q_kernel(qa_ref, cos_ref, sin_ref, wq_ref, wuk_ref, o_ref, *, nope, rope, scale):
    q = _dot(qa_ref[...], wq_ref[...])
    qn = q[:, :nope].astype(BF16)
    qr = (q[:, nope:nope + rope] * cos_ref[...] + q[:, nope + rope:] * sin_ref[...]) * scale
    qlat = _dot_nt(qn, wuk_ref[...]) * scale
    o_ref[0] = jnp.concatenate([qlat, qr], axis=1).astype(BF16)


def sample_q(qa, cos, sin, wq, wuk, *, heads, scale):
    t, q_lora = qa.shape
    rope = cos.shape[1]
    kv_lora = wuk.shape[0]
    nope = wuk.shape[1] // heads
    fixed = lambda h: (0, 0)
    head = lambda h: (0, h)
    kern = functools.partial(_sample_q_kernel, nope=nope, rope=rope, scale=scale)
    return pl.pallas_call(
        kern,
        grid=(heads,),
        in_specs=[pl.BlockSpec((t, q_lora), fixed), pl.BlockSpec((t, rope), fixed),
                  pl.BlockSpec((t, rope), fixed),
                  pl.BlockSpec((q_lora, nope + 2 * rope), head), pl.BlockSpec((kv_lora, nope), head)],
        out_specs=pl.BlockSpec((1, t, kv_lora + rope), lambda h: (h, 0, 0)),
        out_shape=jax.ShapeDtypeStruct((heads, t, kv_lora + rope), BF16),
        compiler_params=_params("parallel"),
        name="sample_q",
    )(qa, cos, sin, wq, wuk)


DECODE_SLOTS = 3


def _decode_kernel(pt_ref, q_ref, cn_ref, rn_ref, cc_hbm, crt_hbm, o_ref, cbuf, rtbuf, sem, m_ref, l_ref, acc_ref,
                   *, layer, pages, heads):
    ci = pl.program_id(1)
    n_chunks = pl.num_programs(1)
    step = pl.program_id(0) * n_chunks + ci
    n_steps = pl.num_programs(0) * n_chunks
    page = cc_hbm.shape[2]
    kv_lora = cc_hbm.shape[3]

    def page_copies(s, j):
        slot = s % DECODE_SLOTS
        pg = pt_ref[s * pages + j]
        rows = pl.ds(j * page, page)
        return (pltpu.make_async_copy(cc_hbm.at[layer, pg], cbuf.at[slot, rows], sem.at[0, slot]),
                pltpu.make_async_copy(crt_hbm.at[layer, pg], rtbuf.at[slot, :, rows], sem.at[1, slot]))

    def start_step(s):
        for j in range(pages):
            for cp in page_copies(s, j):
                cp.start()

    @pl.when(step == 0)
    def _():
        for s in range(DECODE_SLOTS - 1):
            start_step(s)

    ahead = step + DECODE_SLOTS - 1

    @pl.when(ahead < n_steps)
    def _():
        start_step(ahead)

    for j in range(pages):
        for cp in page_copies(step, j):
            cp.wait()

    @pl.when(ci == 0)
    def _():
        m_ref[...] = jnp.full(m_ref.shape, -jnp.inf, F32)
        l_ref[...] = jnp.zeros(l_ref.shape, F32)
        acc_ref[...] = jnp.zeros(acc_ref.shape, F32)

    slot = step % DECODE_SLOTS
    q = q_ref[0]
    qlat, qr = q[:, :kv_lora], q[:, kv_lora:]
    kc = cbuf[slot].astype(BF16)
    s = _dot_nt(qlat, kc) + _dot(qr, rtbuf[slot].astype(BF16))
    m = m_ref[...]
    m_new = jnp.maximum(m, jnp.max(s, axis=-1, keepdims=True))
    alpha = jnp.exp(m - m_new)
    p = jnp.exp(s - m_new)
    l_ref[...] = alpha * l_ref[...] + jnp.sum(p, axis=-1, keepdims=True)
    acc_ref[...] = alpha * acc_ref[...] + _dot(p.astype(BF16), kc)
    m_ref[...] = m_new

    @pl.when(ci == pl.num_programs(1) - 1)
    def _():
        n_new = cn_ref.shape[1]
        cn = cn_ref[0].astype(BF16).astype(F32)
        rn = rn_ref[0].astype(BF16).astype(F32)
        qlf, qrf = qlat.astype(F32), qr.astype(F32)
        row = lax.broadcasted_iota(jnp.int32, (q.shape[0], 1), 0)
        cols = []
        for j in range(n_new):
            sj = (jnp.sum(qlf * cn[j:j + 1], axis=-1, keepdims=True)
                  + jnp.sum(qrf * rn[j:j + 1], axis=-1, keepdims=True))
            cols.append(jnp.where(row >= j * heads, sj, -jnp.inf))
        m0 = m_ref[...]
        m1 = m0
        for sj in cols:
            m1 = jnp.maximum(m1, sj)
        a = jnp.exp(m0 - m1)
        l1 = a * l_ref[...]
        acc1 = a * acc_ref[...]
        for j, sj in enumerate(cols):
            pj = jnp.exp(sj - m1)
            l1 = l1 + pj
            acc1 = acc1 + pj.astype(BF16).astype(F32) * cn[j:j + 1]
        o_ref[0] = acc1 / l1


def decode_attention(page_table, qs, c_new, r_new, cache_c, cache_rt, *, layer, pages, heads):
    n, rows, dq = qs.shape
    n_pages = page_table.shape[1]
    _, _, page, kv_lora = cache_c.shape
    rope = cache_rt.shape[2]
    n_new = c_new.shape[1]
    n_chunks = n_pages // pages
    assert n_pages % pages == 0 and n * n_chunks >= DECODE_SLOTS - 1
    pt = page_table.reshape(-1)
    seq = lambda i, c, pt_ref: (i, 0, 0)
    grid_spec = pltpu.PrefetchScalarGridSpec(
        num_scalar_prefetch=1,
        grid=(n, n_chunks),
        in_specs=[pl.BlockSpec((1, rows, dq), seq), pl.BlockSpec((1, n_new, kv_lora), seq),
                  pl.BlockSpec((1, n_new, rope), seq),
                  pl.BlockSpec(memory_space=pl.ANY), pl.BlockSpec(memory_space=pl.ANY)],
        out_specs=pl.BlockSpec((1, rows, kv_lora), seq),
        scratch_shapes=[pltpu.VMEM((DECODE_SLOTS, pages * page, kv_lora), F32),
                        pltpu.VMEM((DECODE_SLOTS, rope, pages * page), F32),
                        pltpu.SemaphoreType.DMA((2, DECODE_SLOTS)),
                        pltpu.VMEM((rows, 1), F32), pltpu.VMEM((rows, 1), F32),
                        pltpu.VMEM((rows, kv_lora), F32)],
    )
    return pl.pallas_call(
        functools.partial(_decode_kernel, layer=layer, pages=pages, heads=heads),
        grid_spec=grid_spec,
        out_shape=jax.ShapeDtypeStruct((n, rows, kv_lora), F32),
        compiler_params=_params("arbitrary", "arbitrary"),
        name="decode_attention",
    )(pt, qs, c_new, r_new, cache_c, cache_rt)


def _sample_uv_kernel(ol_ref, wuv_ref, o_ref):
    o_ref[...] = _dot(ol_ref[0].astype(BF16), wuv_ref[...]).astype(o_ref.dtype)


def sample_uv(o_lat, wuv):
    heads, t, kv_lora = o_lat.shape
    vdim = wuv.shape[1] // heads
    return pl.pallas_call(
        _sample_uv_kernel,
        grid=(heads,),
        in_specs=[pl.BlockSpec((1, t, kv_lora), lambda h: (h, 0, 0)), pl.BlockSpec((kv_lora, vdim), lambda h: (0, h))],
        out_specs=pl.BlockSpec((t, vdim), lambda h: (0, h)),
        out_shape=jax.ShapeDtypeStruct((t, heads * vdim), BF16),
        compiler_params=_params("parallel"),
        name="sample_uv",
    )(o_lat, wuv)


def _rot_cols(w, half):
    return jnp.concatenate([-w[..., half:], w[..., :half]], axis=-1)


def _rope_tables(pos, rope):
    inv_freq = ROPE_BASE ** (-jnp.arange(0, rope, 2, dtype=F32) / rope)
    ang = pos.astype(F32)[:, None] * inv_freq[None, :]
    cos, sin = jnp.cos(ang), jnp.sin(ang)
    return jnp.concatenate([cos, cos], axis=-1), jnp.concatenate([sin, sin], axis=-1)


def kernel(x_prompt, x_sample, cache_kv_latent, cache_k_rope, state_conv, state_ffn_conv, page_table, norm_mix_pre, norm_mix_post, norm_ffn_pre, norm_ffn_post, cv_w_in, cv_b_in, cv_dw, cv_dw_b, cv_ln_g, cv_ln_b, cv_w_out, cv_b_out, mla_w_in, mla_q_norm, mla_kv_norm, mla_w_qb, mla_w_uk, mla_w_uv, mla_w_o, ffn_w_gate, ffn_w_up, ffn_dw, ffn_dw_b, ffn_w_down):
    batch, seq, d = x_prompt.shape
    n_dec, l_dec, _ = x_sample.shape
    depth = norm_mix_pre.shape[0]
    conv_w = cv_dw.shape[1]
    ffn_w = ffn_dw.shape[1]
    heads, nope = mla_w_uk.shape[2], mla_w_uk.shape[3]
    q_lora, kv_lora = mla_q_norm.shape[1], mla_kv_norm.shape[1]
    rope = mla_w_in.shape[2] - q_lora - kv_lora
    vdim = mla_w_uv.shape[3]
    dff = ffn_w_gate.shape[2]
    past = page_table.shape[1] * cache_kv_latent.shape[2]
    scale = 1.0 / math.sqrt(nope + rope)
    tp = batch * seq
    ts = n_dec * l_dec

    tm = min(512, seq)
    tm_ffn = min(1024, seq)
    tf = 512 if dff % 512 == 0 else dff
    tn = min(1024, d)
    tm_cin = min(1024, seq)
    tm_conv = min(256, seq)
    pages = min(32, page_table.shape[1])
    flash_heads = 2
    tq_flash = min(1024, seq)
    tk_flash = min(1024, seq)

    vec = lambda a: a[:, None, :]
    n_pre, n_post, f_pre, f_post = vec(norm_mix_pre), vec(norm_mix_post), vec(norm_ffn_pre), vec(norm_ffn_post)
    bf = lambda a: a.astype(BF16)
    cv_w_in_b, cv_w_out_b = bf(cv_w_in), bf(cv_w_out)
    w_gate_b, w_up_b, w_down_b = bf(ffn_w_gate), bf(ffn_w_up), bf(ffn_w_down)

    cache_rt = cache_k_rope.transpose(0, 1, 3, 2)
    state_conv_t = state_conv.transpose(0, 2, 1, 3)
    state_ffn_t = state_ffn_conv.transpose(0, 2, 1, 3)

    cos_p, sin_p = _rope_tables(jnp.tile(jnp.arange(seq), batch), rope)
    cos_s, sin_s = _rope_tables(jnp.repeat(past + jnp.arange(l_dec), n_dec), rope)

    xp = x_prompt.reshape(tp, d)
    kv_p, kr_p, conv_p, ffn_p = [], [], [], []
    xs = x_sample.transpose(1, 0, 2).reshape(ts, d)
    kv_s, kr_s, conv_s, ffn_s = [], [], [], []

    for i in range(depth):
        j = i // 2
        if i % 2 == 0:
            b_in = cv_b_in[j][None, :]
            args = (cv_dw[j], cv_dw_b[j][None, :], cv_ln_g[j][None, :], cv_ln_b[j][None, :], cv_w_out_b,
                    cv_b_out[j][None, :], n_post[i])
            u = conv_in(xp, n_pre[i], cv_w_in_b, b_in, layer=j, tm=tm_cin, tn=tn)
            conv_p.append(u.reshape(batch, seq, d)[:, seq - (conv_w - 1):])
            xp = conv_out_prompt(u, xp, *args, layer=j, seq=seq, tm=tm_conv)

            us = conv_in(xs, n_pre[i], cv_w_in_b, b_in, layer=j, tm=ts, tn=tn)
            xs, new_state = conv_out_sample(state_conv_t, us.reshape(l_dec, n_dec, d), xs.reshape(l_dec, n_dec, d),
                                            *args, layer=j, ns=16)
            xs = xs.reshape(ts, d)
            conv_s.append(new_state.transpose(1, 0, 2))
        else:
            w_in = mla_w_in[j]
            w_in_ext = bf(jnp.concatenate([w_in, _rot_cols(w_in[:, q_lora + kv_lora:], rope // 2)], axis=-1))
            w_qb = mla_w_qb[j]
            wq = bf(jnp.concatenate([w_qb, _rot_cols(w_qb[..., nope:], rope // 2)], axis=-1))
            wq = wq.reshape(q_lora, heads * (nope + 2 * rope))
            wuk = bf(mla_w_uk[j]).reshape(kv_lora, heads * nope)
            wuv = bf(mla_w_uv[j]).reshape(kv_lora, heads * vdim)
            w_o = bf(mla_w_o[j]).reshape(heads * vdim, d)
            w = (w_in_ext, mla_q_norm[j][None, :], mla_kv_norm[j][None, :])

            qa, ckv, kr = mla_in(xp, n_pre[i], *w, cos_p, sin_p, tm=tm)
            kv_p.append(ckv.reshape(batch, seq, kv_lora))
            kr_p.append(kr.reshape(batch, seq, rope))
            q, k, v = mla_qkv(qa, ckv, kr, cos_p, sin_p, wq, wuk, wuv, heads=heads, tm=tm, scale=scale)
            o = flash_prompt(q.reshape(batch, seq, -1), k.reshape(batch, seq, -1), v.reshape(batch, seq, -1),
                             heads=heads, tq=tq_flash, tk=tk_flash, hb=flash_heads)
            xp = proj_post(o.reshape(tp, heads * vdim), xp, w_o, n_post[i], tm=tm)

            qa, ckv, kr = mla_in(xs, n_pre[i], *w, cos_s, sin_s, tm=ts)
            c_new = ckv.reshape(l_dec, n_dec, kv_lora).transpose(1, 0, 2)
            r_new = kr.reshape(l_dec, n_dec, rope).transpose(1, 0, 2)
            kv_s.append(c_new)
            kr_s.append(r_new)
            qs = sample_q(qa, cos_s, sin_s, wq, wuk, heads=heads, scale=scale)
            qs = qs.reshape(heads, l_dec, n_dec, kv_lora + rope).transpose(2, 1, 0, 3)
            qs = qs.reshape(n_dec, l_dec * heads, kv_lora + rope)
            o_lat = decode_attention(page_table, qs, c_new, r_new, cache_kv_latent, cache_rt,
                                     layer=j, pages=pages, heads=heads)
            o_lat = o_lat.reshape(n_dec, l_dec, heads, kv_lora).transpose(2, 1, 0, 3).reshape(heads, ts, kv_lora)
            xs = proj_post(sample_uv(o_lat, wuv), xs, w_o, n_post[i], tm=ts)

        ffn_w_args = (w_gate_b, w_up_b, ffn_dw[i], ffn_dw_b[i][None, :], w_down_b)
        xp, g_tail = ffn_prompt(xp, f_pre[i], f_post[i], *ffn_w_args, layer=i, seq=seq, tm=tm_ffn, tf=tf)
        g_tail = g_tail.reshape(batch, seq // tm_ffn, FFN_TAIL, dff)
        ffn_p.append(g_tail[:, -1, FFN_TAIL - (ffn_w - 1):])
        xs, g_new = ffn_sample(xs, f_pre[i], f_post[i], state_ffn_t, *ffn_w_args,
                               layer=i, tf=tf)
        ffn_s.append(g_new.transpose(1, 0, 2))

    y_prompt = xp.reshape(batch, seq, d)
    y_sample = xs.reshape(l_dec, n_dec, d).transpose(1, 0, 2)
    return (y_prompt, y_sample, jnp.stack(kv_p), jnp.stack(kr_p), jnp.stack(kv_s), jnp.stack(kr_s),
            jnp.stack(conv_p), jnp.stack(conv_s), jnp.stack(ffn_p), jnp.stack(ffn_s))
```
